```python
import jax, jax.numpy as jnp
from jax import lax
import numpy as np

D_MODEL = 4096
BATCH = 2
SEQ = 4096
DEPTH = 2

GRID_W = 64
CTX_LEN = 256
N_MOD = 6
EPS = 1e-6
CONV_WIDTH = D_MODEL // 2
CONV_GROUPS = 16
CONV_K = 31
CONV_PAD = (CONV_K - 1) // 2
LRU_WIDTH = D_MODEL // 2
LRU_HEADS = 16
LRU_HEAD_DIM = LRU_WIDTH // LRU_HEADS
LRU_CONV_K = 4
LRU_C = 8.0
N_DIR = 2
MIX_WIDTH = CONV_WIDTH + LRU_WIDTH
IN_WIDTH = 2 * CONV_WIDTH + 2 * LRU_WIDTH
N_EXPERTS = 32
TOP_K = 4
D_EXPERT = D_MODEL // 8
SWIGLU_LIMIT = 7.0
SWIGLU_ALPHA = 1.702

kernel_name = 'hymba_conformer_rglru_moe_dit'


def rms_unit(x):
    xf = x.astype(jnp.float32)
    return (xf * lax.rsqrt(jnp.mean(xf * xf, axis=-1, keepdims=True) + EPS)).astype(x.dtype)


def rmsnorm(x, g):
    return rms_unit(x) * g


def modulate(h, shift, scale):
    return h * (1 + scale) + shift


def adaln(cvec, w, b):
    m = (jax.nn.silu(cvec) @ w + b)[:, None, :]
    return jnp.split(m, N_MOD, axis=-1)


def depthwise_conv(x, w, b, pad_lo, pad_hi):
    y = lax.conv_general_dilated(x, w[:, None, :], window_strides=(1,), padding=[(pad_lo, pad_hi)],
                                 dimension_numbers=('NWC', 'WIO', 'NWC'), feature_group_count=x.shape[-1])
    return y + b


def group_norm(u, g, b):
    B_, L_, C = u.shape
    uf = u.astype(jnp.float32).reshape(B_, L_, CONV_GROUPS, C // CONV_GROUPS)
    mu = jnp.mean(uf, axis=-1, keepdims=True)
    var = jnp.mean(jnp.square(uf - mu), axis=-1, keepdims=True)
    y = ((uf - mu) * lax.rsqrt(var + EPS)).reshape(B_, L_, C).astype(u.dtype)
    return y * g + b


def conv_module(a, gt, w, b, gn_g, gn_b, rows):
    u = a * jax.nn.sigmoid(gt)
    if rows is None:
        v = depthwise_conv(u, w, b, CONV_PAD, CONV_PAD)
    else:
        B_, L_, C = u.shape
        v = depthwise_conv(u.reshape(B_ * rows, GRID_W, C), w, b, CONV_PAD, CONV_PAD).reshape(B_, L_, C)
    return jax.nn.silu(group_norm(v, gn_g, gn_b))


def _linear_combine(e1, e2):
    a1, b1 = e1
    a2, b2 = e2
    return a1 * a2, a2 * b1 + b2


def rg_lru_scan(xc, w_a, b_a, w_x, b_x, lam, h0):
    B_, L_, W = xc.shape
    xf = xc.astype(jnp.float32)
    xh = xf.reshape(B_, L_, LRU_HEADS, LRU_HEAD_DIM)
    r = jax.nn.sigmoid(jnp.einsum('blhi,hij->blhj', xh, w_a.astype(jnp.float32)).reshape(B_, L_, W) + b_a.astype(jnp.float32))
    i = jax.nn.sigmoid(jnp.einsum('blhi,hij->blhj', xh, w_x.astype(jnp.float32)).reshape(B_, L_, W) + b_x.astype(jnp.float32))
    log_a = -LRU_C * jax.nn.softplus(-lam.astype(jnp.float32)) * r
    a = jnp.exp(log_a)
    bterm = jnp.sqrt(-jnp.expm1(2.0 * log_a)) * (i * xf)
    a_cum, h = lax.associative_scan(_linear_combine, (a, bterm), axis=1)
    h = h + a_cum * h0[:, None, :]
    return h, h[:, -1]


def lru_direction(xin, conv_w, conv_b, w_a, b_a, w_x, b_x, lam, h0, reverse):
    if reverse:
        xin = jnp.flip(xin, axis=1)
    xc = depthwise_conv(xin, conv_w, conv_b, LRU_CONV_K - 1, 0)
    h, h_last = rg_lru_scan(xc, w_a, b_a, w_x, b_x, lam, h0)
    if reverse:
        h = jnp.flip(h, axis=1)
    return h, h_last


def recurrent_out(gate, hf, hb):
    return jax.nn.gelu(gate, approximate=True) * (hf + hb).astype(gate.dtype)


def merge_groups(y_conv, y_lru, g):
    return jnp.concatenate([rms_unit(y_conv), rms_unit(y_lru)], axis=-1) * g


def moe(h, router_w, router_b, w_gu, b_gu, w_down, b_down):
    logits = h.astype(jnp.float32) @ router_w.astype(jnp.float32) + router_b.astype(jnp.float32)
    top_v, top_i = lax.top_k(logits, TOP_K)
    p = jax.nn.softmax(top_v, axis=-1)
    comb = jnp.sum(jax.nn.one_hot(top_i, N_EXPERTS, dtype=jnp.float32) * p[..., None], axis=-2)
    y = jnp.zeros(h.shape, jnp.float32)
    for e in range(N_EXPERTS):
        gu = h @ w_gu[e] + b_gu[e]
        gate = jnp.minimum(gu[:, 0::2], SWIGLU_LIMIT)
        up = jnp.clip(gu[:, 1::2], -SWIGLU_LIMIT, SWIGLU_LIMIT)
        act = (up + 1) * gate * jax.nn.sigmoid(SWIGLU_ALPHA * gate)
        out = act @ w_down[e] + b_down[e]
        y = y + comb[:, e:e + 1] * out.astype(jnp.float32)
    return y.astype(h.dtype)


def setup_inputs(seed: int = 0) -> dict:
    key = jax.random.key(seed)
    k = jax.random.split(key, 32)

    def nrm(kk, shape, s):
        return jax.random.normal(kk, shape, jnp.float32) * s

    u = jax.random.uniform(k[20], (DEPTH, N_DIR, LRU_WIDTH), jnp.float32, minval=0.9, maxval=0.999)
    a0 = u ** (1.0 / LRU_C)
    lam = jnp.log(a0) - jnp.log1p(-a0)
    return {
        'x': nrm(k[0], (BATCH, SEQ, D_MODEL), 1.0),
        'c': nrm(k[1], (BATCH, D_MODEL), 1.0),
        'ctx': nrm(k[2], (BATCH, CTX_LEN, D_MODEL), 1.0),
        'c_ctx': nrm(k[3], (D_MODEL,), 1.0),
        'norm1_g': 1.0 + nrm(k[4], (DEPTH, D_MODEL), 0.05),
        'norm2_g': 1.0 + nrm(k[5], (DEPTH, D_MODEL), 0.05),
        'w_ada': nrm(k[6], (DEPTH, D_MODEL, N_MOD * D_MODEL), 0.5 * D_MODEL ** -0.5),
        'b_ada': nrm(k[7], (DEPTH, N_MOD * D_MODEL), 0.02),
        'w_in': nrm(k[8], (DEPTH, D_MODEL, IN_WIDTH), D_MODEL ** -0.5),
        'conv_w': nrm(k[9], (DEPTH, CONV_K, CONV_WIDTH), CONV_K ** -0.5),
        'conv_b': nrm(k[10], (DEPTH, CONV_WIDTH), 0.02),
        'gn_g': 1.0 + nrm(k[11], (DEPTH, CONV_WIDTH), 0.05),
        'gn_b': nrm(k[12], (DEPTH, CONV_WIDTH), 0.02),
        'lru_conv_w': nrm(k[13], (DEPTH, N_DIR, LRU_CONV_K, LRU_WIDTH), LRU_CONV_K ** -0.5),
        'lru_conv_b': nrm(k[14], (DEPTH, N_DIR, LRU_WIDTH), 0.02),
        'lru_w_a': nrm(k[15], (DEPTH, N_DIR, LRU_HEADS, LRU_HEAD_DIM, LRU_HEAD_DIM), LRU_HEAD_DIM ** -0.5),
        'lru_b_a': nrm(k[16], (DEPTH, N_DIR, LRU_WIDTH), 0.02),
        'lru_w_x': nrm(k[17], (DEPTH, N_DIR, LRU_HEADS, LRU_HEAD_DIM, LRU_HEAD_DIM), LRU_HEAD_DIM ** -0.5),
        'lru_b_x': nrm(k[18], (DEPTH, N_DIR, LRU_WIDTH), 0.02),
        'lru_lambda': lam,
        'mix_norm_g': 1.0 + nrm(k[19], (DEPTH, MIX_WIDTH), 0.05),
        'w_out': nrm(k[21], (DEPTH, MIX_WIDTH, D_MODEL), MIX_WIDTH ** -0.5),
        'router_w': nrm(k[22], (DEPTH, D_MODEL, N_EXPERTS), D_MODEL ** -0.5),
        'router_b': nrm(k[23], (DEPTH, N_EXPERTS), 0.01),
        'exp_w_gu': nrm(k[24], (DEPTH, N_EXPERTS, D_MODEL, 2 * D_EXPERT), D_MODEL ** -0.5),
        'exp_b_gu': nrm(k[25], (DEPTH, N_EXPERTS, 2 * D_EXPERT), 0.02),
        'exp_w_down': nrm(k[26], (DEPTH, N_EXPERTS, D_EXPERT, D_MODEL), D_EXPERT ** -0.5),
        'exp_b_down': nrm(k[27], (DEPTH, N_EXPERTS, D_MODEL), 0.02),
        'final_norm_g': 1.0 + nrm(k[28], (D_MODEL,), 0.05),
    }


def reference(x, c, ctx, c_ctx, norm1_g, norm2_g, w_ada, b_ada, w_in, conv_w, conv_b, gn_g, gn_b,
              lru_conv_w, lru_conv_b, lru_w_a, lru_b_a, lru_w_x, lru_b_x, lru_lambda, mix_norm_g, w_out,
              router_w, router_b, exp_w_gu, exp_b_gu, exp_w_down, exp_b_down, final_norm_g):
    Bsz, L = x.shape[0], x.shape[1]
    rows = L // GRID_W
    s1, s2 = CONV_WIDTH, 2 * CONV_WIDTH
    s3 = s2 + LRU_WIDTH
    h0 = jnp.zeros((Bsz, LRU_WIDTH), jnp.float32)
    ctx_s = ctx
    for l in range(DEPTH):
        last = l == DEPTH - 1
        sh1, sc1, g1, sh2, sc2, g2 = adaln(c, w_ada[l], b_ada[l])
        csh1, csc1, cg1, csh2, csc2, cg2 = adaln(c_ctx[None], w_ada[l], b_ada[l])

        def lp(d):
            return (lru_conv_w[l, d], lru_conv_b[l, d], lru_w_a[l, d], lru_b_a[l, d],
                    lru_w_x[l, d], lru_b_x[l, d], lru_lambda[l, d])

        hl = modulate(rmsnorm(x, norm1_g[l]), sh1, sc1)
        hc = modulate(rmsnorm(ctx_s, norm1_g[l]), csh1, csc1)
        zl = hl @ w_in[l]
        cv_a, cv_g, lr_x, lr_g = zl[..., :s1], zl[..., s1:s2], zl[..., s2:s3], zl[..., s3:]
        if last:
            clr_x = hc @ w_in[l][:, s2:s3]
        else:
            zc = hc @ w_in[l]
            ccv_a, ccv_g, clr_x, clr_g = zc[..., :s1], zc[..., s1:s2], zc[..., s2:s3], zc[..., s3:]
        hcf, hcf_last = lru_direction(clr_x, *lp(0), h0, False)
        hcb, hcb_last = lru_direction(clr_x, *lp(1), h0, True)
        hlf, _ = lru_direction(lr_x, *lp(0), hcf_last, False)
        hlb, _ = lru_direction(lr_x, *lp(1), hcb_last, True)
        lat_mix = merge_groups(conv_module(cv_a, cv_g, conv_w[l], conv_b[l], gn_g[l], gn_b[l], rows),
                               recurrent_out(lr_g, hlf, hlb), mix_norm_g[l])
        x = x + g1 * (lat_mix @ w_out[l])
        if not last:
            ctx_mix = merge_groups(conv_module(ccv_a, ccv_g, conv_w[l], conv_b[l], gn_g[l], gn_b[l], None),
                                   recurrent_out(clr_g, hcf, hcb), mix_norm_g[l])
            ctx_s = ctx_s + cg1 * (ctx_mix @ w_out[l])

        hl2 = modulate(rmsnorm(x, norm2_g[l]), sh2, sc2)
        moe_args = (router_w[l], router_b[l], exp_w_gu[l], exp_b_gu[l], exp_w_down[l], exp_b_down[l])
        if last:
            x = x + g2 * moe(hl2.reshape(-1, D_MODEL), *moe_args).reshape(x.shape)
        else:
            hc2 = modulate(rmsnorm(ctx_s, norm2_g[l]), csh2, csc2)
            n_ctx = hc2.shape[0] * hc2.shape[1]
            tok = jnp.concatenate([hc2.reshape(-1, D_MODEL), hl2.reshape(-1, D_MODEL)], axis=0)
            y = moe(tok, *moe_args)
            ctx_s = ctx_s + cg2 * y[:n_ctx].reshape(ctx_s.shape)
            x = x + g2 * y[n_ctx:].reshape(x.shape)
    return rmsnorm(x, final_norm_g)
```

```python
import jax
import jax.numpy as jnp
from jax import lax
from jax.experimental import pallas as pl
from jax.experimental.pallas import tpu as pltpu

GRID_W = 64
CONV_GROUPS = 16
TOP_K = 4
N_MOD = 6
EPS = 1e-6
LRU_C = 8.0
SWIGLU_LIMIT = 7.0
SWIGLU_ALPHA = 1.702

LANES = 128
SUBLANES = 8
MOD_ROWS = 8
HALO = 16
CONV_ROWS = 64
EXPERT_TILE = 256
COMBINE_TILE = 64
NEG_BIG = -1e30

F32 = jnp.float32
BF16 = jnp.bfloat16
U32 = jnp.uint32
I32 = jnp.int32


def _cp(sems, vmem_mb):
    return pltpu.CompilerParams(dimension_semantics=sems, vmem_limit_bytes=vmem_mb << 20)


def _seg(i, n_ctx_tiles, lat_tiles):
    return jnp.where(i < n_ctx_tiles, 0, 1 + (i - n_ctx_tiles) // lat_tiles)


def _mod_idx(layer, seg, j):
    return (layer * MOD_ROWS + seg) * N_MOD + j


def _dot(a, b):
    return jnp.dot(a, b, preferred_element_type=F32)


def _adaln(cvec, w_ada, b_ada):
    depth, d, n = w_ada.shape
    tn = min(n, 1024)

    def body(c_ref, w_ref, b_ref, o_ref):
        c = c_ref[...]
        s = (c * jax.nn.sigmoid(c)).astype(BF16)
        o_ref[0] = _dot(s, w_ref[0].astype(BF16)) + b_ref[0]

    return pl.pallas_call(
        body,
        grid=(depth, n // tn),
        in_specs=[pl.BlockSpec((MOD_ROWS, d), lambda l, j: (0, 0)),
                  pl.BlockSpec((1, d, tn), lambda l, j: (l, 0, j)),
                  pl.BlockSpec((1, 1, tn), lambda l, j: (l, 0, j))],
        out_specs=pl.BlockSpec((1, MOD_ROWS, tn), lambda l, j: (l, 0, j)),
        out_shape=jax.ShapeDtypeStruct((depth, MOD_ROWS, n), F32),
        compiler_params=_cp(("parallel", "parallel"), 48),
        name="adaln",
    )(cvec, w_ada, b_ada.reshape(depth, 1, n))


def _prenorm(xs, g, mods, layer, dims):
    t, d = xs.shape
    tl, nct, lt = dims["TL"], dims["NCT"], dims["LT"]

    def body(x_ref, g_ref, sh_ref, sc_ref, o_ref):
        x = x_ref[...]
        ms = jnp.mean(x * x, axis=-1, keepdims=True)
        y = x * lax.rsqrt(ms + EPS) * g_ref[...]
        o_ref[...] = (y * (1.0 + sc_ref[0]) + sh_ref[0]).astype(o_ref.dtype)

    return pl.pallas_call(
        body,
        grid=(t // tl,),
        in_specs=[pl.BlockSpec((tl, d), lambda i: (i, 0)),
                  pl.BlockSpec((1, d), lambda i: (0, 0)),
                  pl.BlockSpec((1, 1, d), lambda i: (_mod_idx(layer, _seg(i, nct, lt), 0), 0, 0)),
                  pl.BlockSpec((1, 1, d), lambda i: (_mod_idx(layer, _seg(i, nct, lt), 1), 0, 0))],
        out_specs=pl.BlockSpec((tl, d), lambda i: (i, 0)),
        out_shape=jax.ShapeDtypeStruct((t, d), BF16),
        compiler_params=_cp(("parallel",), 32),
        name="prenorm1",
    )(xs, g.reshape(1, d), mods, mods)


def _matmul(a, w, tm, tn, name):
    m, k = a.shape
    n = w.shape[1]

    def body(a_ref, w_ref, o_ref):
        o_ref[...] = _dot(a_ref[...], w_ref[...])

    return pl.pallas_call(
        body,
        grid=(n // tn, m // tm),
        in_specs=[pl.BlockSpec((tm, k), lambda j, i: (i, 0)),
                  pl.BlockSpec((k, tn), lambda j, i: (0, j))],
        out_specs=pl.BlockSpec((tm, tn), lambda j, i: (i, j)),
        out_shape=jax.ShapeDtypeStruct((m, n), F32),
        compiler_params=_cp(("parallel", "parallel"), 48),
        name=name,
    )(a, w)


def _conv_module(zl, conv_w, conv_b, gn_g, gn_b, mix_g, dims):
    t = zl.shape[0]
    k_taps, cw = conv_w.shape
    pad = (k_taps - 1) // 2
    tl, nct, ctx_len = dims["TL"], dims["NCT"], dims["CTX"]
    assert cw // CONV_GROUPS == LANES and pad <= HALO
    assert tl % GRID_W == 0 and GRID_W % CONV_ROWS == 0 and ctx_len % CONV_ROWS == 0
    upad_rows = max((tl // GRID_W) * (GRID_W + 2 * HALO), ctx_len + 2 * HALO)

    def body(za_ref, zg_ref, w_ref, b_ref, gg_ref, gb_ref, mg_ref, o_ref, upad, ybuf, ssbuf):
        i = pl.program_id(0)
        upad[...] = jnp.zeros_like(upad)
        ssbuf[...] = jnp.zeros_like(ssbuf)

        def path(seg_len):
            nseg = tl // seg_len
            stride = seg_len + 2 * HALO

            def chunk(c, carry):
                ln = pl.ds(pl.multiple_of(c * LANES, LANES), LANES)
                u = za_ref[:, ln] * jax.nn.sigmoid(zg_ref[:, ln])
                for s in range(nseg):
                    upad[s * stride + HALO:s * stride + HALO + seg_len, :] = u[s * seg_len:(s + 1) * seg_len]
                bias = b_ref[:, ln]
                gg = gg_ref[:, ln]
                gb = gb_ref[:, ln]
                for s in range(nseg):
                    for r0 in range(0, seg_len, CONV_ROWS):
                        base = s * stride + HALO + r0 - pad
                        acc = jnp.broadcast_to(bias, (CONV_ROWS, LANES))
                        for k in range(k_taps):
                            acc = acc + w_ref[pl.ds(k, 1), ln] * upad[base + k:base + k + CONV_ROWS, :]
                        mu = jnp.mean(acc, axis=-1, keepdims=True)
                        dv = acc - mu
                        var = jnp.mean(dv * dv, axis=-1, keepdims=True)
                        y = dv * lax.rsqrt(var + EPS) * gg + gb
                        y = y * jax.nn.sigmoid(y)
                        rows = pl.ds(s * seg_len + r0, CONV_ROWS)
                        ybuf[rows, ln] = y
                        ssbuf[rows, :] = ssbuf[rows, :] + jnp.sum(y * y, axis=-1, keepdims=True)
                return carry

            lax.fori_loop(0, cw // LANES, chunk, 0)

        @pl.when(i < nct)
        def _():
            path(ctx_len)

        @pl.when(i >= nct)
        def _():
            path(GRID_W)

        inv = lax.rsqrt(ssbuf[:, 0:1] * (1.0 / cw) + EPS)
        o_ref[...] = (ybuf[...] * inv * mg_ref[...]).astype(o_ref.dtype)

    vec = pl.BlockSpec((1, cw), lambda i: (0, 0))
    return pl.pallas_call(
        body,
        grid=(t // tl,),
        in_specs=[pl.BlockSpec((tl, cw), lambda i: (i, 0)),
                  pl.BlockSpec((tl, cw), lambda i: (i, 1)),
                  pl.BlockSpec((k_taps, cw), lambda i: (0, 0)),
                  vec, vec, vec, vec],
        out_specs=pl.BlockSpec((tl, cw), lambda i: (i, 0)),
        out_shape=jax.ShapeDtypeStruct((t, cw), BF16),
        scratch_shapes=[pltpu.VMEM((upad_rows, LANES), F32),
                        pltpu.VMEM((tl, cw), F32),
                        pltpu.VMEM((tl, LANES), F32)],
        compiler_params=_cp(("parallel",), 40),
        name="conv_module",
    )(zl, zl, conv_w, conv_b.reshape(1, cw), gn_g.reshape(1, cw), gn_b.reshape(1, cw), mix_g.reshape(1, cw))


def _lru_gates(zl, cw, cb, wa, ba, wx, bx, lam, dims):
    t = zl.shape[0]
    ndir, kc, lw = cw.shape
    heads, hd = wa.shape[1], wa.shape[2]
    tl, nct, lt = dims["TL"], dims["NCT"], dims["LT"]
    assert hd == LANES and heads * hd == lw and ndir == 2 and kc - 1 <= SUBLANES
    nrb = t // SUBLANES
    tls = tl // SUBLANES

    def body(x_ref, p_ref, n_ref, cw_ref, cb_ref, wa_ref, ba_ref, wx_ref, bx_ref, lam_ref,
             af_ref, bf_ref, ab_ref, bb_ref, xp):
        i = pl.program_id(0)
        is_ctx = i < nct
        j = jnp.where(is_ctx, 0, (i - nct) % lt)
        first = jnp.logical_or(is_ctx, j == 0)
        last = jnp.logical_or(is_ctx, j == lt - 1)
        outs = ((af_ref, bf_ref), (ab_ref, bb_ref))

        def head(h, carry):
            ln = pl.ds(pl.multiple_of(h * LANES, LANES), LANES)
            xp[0:SUBLANES, :] = jnp.where(first, 0.0, p_ref[:, ln])
            xp[SUBLANES:SUBLANES + tl, :] = x_ref[:, ln]
            xp[SUBLANES + tl:2 * SUBLANES + tl, :] = jnp.where(last, 0.0, n_ref[:, ln])
            win = {o: xp[SUBLANES + o:SUBLANES + o + tl, :] for o in range(-(kc - 1), kc)}
            for d in range(ndir):
                xc = jnp.broadcast_to(cb_ref[pl.ds(d, 1), ln], (tl, LANES))
                for k in range(kc):
                    off = k - (kc - 1) if d == 0 else (kc - 1) - k
                    xc = xc + cw_ref[d, pl.ds(k, 1), ln] * win[off]
                xcb = xc.astype(BF16)
                r = jax.nn.sigmoid(_dot(xcb, wa_ref[d, h].astype(BF16)) + ba_ref[pl.ds(d, 1), ln])
                ig = jax.nn.sigmoid(_dot(xcb, wx_ref[d, h].astype(BF16)) + bx_ref[pl.ds(d, 1), ln])
                sp = jax.nn.softplus(-lam_ref[pl.ds(d, 1), ln])
                log_a = (-LRU_C * sp) * r
                a_ref, b_ref = outs[d]
                a = jnp.exp(log_a)
                a_ref[:, ln] = a
                b_ref[:, ln] = jnp.sqrt(1.0 - a * a) * (ig * xc)
            return carry

        lax.fori_loop(0, heads, head, 0)

    xcol = 2
    full2 = pl.BlockSpec((ndir, lw), lambda i: (0, 0))
    full4 = pl.BlockSpec((ndir, heads, hd, hd), lambda i: (0, 0, 0, 0))
    out = pl.BlockSpec((tl, lw), lambda i: (i, 0))
    shp = jax.ShapeDtypeStruct((t, lw), F32)
    return pl.pallas_call(
        body,
        grid=(t // tl,),
        in_specs=[pl.BlockSpec((tl, lw), lambda i: (i, xcol)),
                  pl.BlockSpec((SUBLANES, lw), lambda i: (jnp.maximum(i * tls - 1, 0), xcol)),
                  pl.BlockSpec((SUBLANES, lw), lambda i: (jnp.minimum((i + 1) * tls, nrb - 1), xcol)),
                  pl.BlockSpec((ndir, kc, lw), lambda i: (0, 0, 0)),
                  full2, full4, full2, full4, full2, full2],
        out_specs=[out, out, out, out],
        out_shape=[shp, shp, shp, shp],
        scratch_shapes=[pltpu.VMEM((tl + 2 * SUBLANES, LANES), F32)],
        compiler_params=_cp(("parallel",), 40),
        name="lru_gates",
    )(zl, zl, zl, cw, cb, wa, ba, wx, bx, lam)


def _scan(af, bf, ab, bb, dims):
    t, lw = af.shape
    hs = lw // LANES
    tl, nct, lt, nb = dims["TL"], dims["NCT"], dims["LT"], dims["B"]
    assert nct == nb

    def v(z):
        return z.reshape(t, hs, LANES)

    def fidx(b, s):
        return jnp.where(s == 0, b, nct + b * lt + s - 1)

    def bidx(b, s):
        return jnp.where(s == 0, b, nct + b * lt + lt - s)

    def body(af_ref, bf_ref, ab_ref, bb_ref, hf_ref, hb_ref, hc):
        @pl.when(pl.program_id(1) == 0)
        def _():
            hc[...] = jnp.zeros_like(hc)

        def step(tt, carry):
            hf, hb = carry
            tb = tl - 1 - tt
            hf = af_ref[tt] * hf + bf_ref[tt]
            hf_ref[tt] = hf
            hb = ab_ref[tb] * hb + bb_ref[tb]
            hb_ref[tb] = hb
            return hf, hb

        hf, hb = lax.fori_loop(0, tl, step, (hc[0], hc[1]), unroll=8)
        hc[0] = hf
        hc[1] = hb

    fspec = pl.BlockSpec((tl, hs, LANES), lambda b, s: (fidx(b, s), 0, 0))
    bspec = pl.BlockSpec((tl, hs, LANES), lambda b, s: (bidx(b, s), 0, 0))
    shp = jax.ShapeDtypeStruct((t, hs, LANES), F32)
    hf, hb = pl.pallas_call(
        body,
        grid=(nb, 1 + lt),
        in_specs=[fspec, fspec, bspec, bspec],
        out_specs=[fspec, bspec],
        out_shape=[shp, shp],
        scratch_shapes=[pltpu.VMEM((2, hs, LANES), F32)],
        compiler_params=_cp(("parallel", "arbitrary"), 40),
        name="lru_scan",
    )(v(af), v(bf), v(ab), v(bb))
    return hf.reshape(t, lw), hb.reshape(t, lw)


def _lru_out(zl, hf, hb, mix_g, dims):
    t, lw = hf.shape
    tl = dims["TL"]
    gcol = 3

    def body(g_ref, hf_ref, hb_ref, mg_ref, o_ref):
        y = jax.nn.gelu(g_ref[...], approximate=True) * (hf_ref[...] + hb_ref[...])
        inv = lax.rsqrt(jnp.mean(y * y, axis=-1, keepdims=True) + EPS)
        o_ref[...] = (y * inv * mg_ref[...]).astype(o_ref.dtype)

    row = pl.BlockSpec((tl, lw), lambda i: (i, 0))
    return pl.pallas_call(
        body,
        grid=(t // tl,),
        in_specs=[pl.BlockSpec((tl, lw), lambda i: (i, gcol)), row, row,
                  pl.BlockSpec((1, lw), lambda i: (0, 0))],
        out_specs=row,
        out_shape=jax.ShapeDtypeStruct((t, lw), BF16),
        compiler_params=_cp(("parallel",), 40),
        name="lru_out",
    )(zl, hf, hb, mix_g.reshape(1, lw))


def _wout_residual(mix_c, mix_r, w_out, xs, mods, layer, dims):
    t, d = xs.shape
    kc = mix_c.shape[1]
    kr = mix_r.shape[1]
    assert kc == kr
    tm, tn = dims["TM"], min(d, 1024)
    nct, lt = dims["B"] * dims["CTX"] // tm, dims["L"] // tm

    def body(a1_ref, a2_ref, w1_ref, w2_ref, x_ref, g_ref, o_ref):
        acc = _dot(a1_ref[...], w1_ref[...]) + _dot(a2_ref[...], w2_ref[...])
        o_ref[...] = x_ref[...] + g_ref[0] * acc

    return pl.pallas_call(
        body,
        grid=(d // tn, t // tm),
        in_specs=[pl.BlockSpec((tm, kc), lambda j, i: (i, 0)),
                  pl.BlockSpec((tm, kr), lambda j, i: (i, 0)),
                  pl.BlockSpec((kc, tn), lambda j, i: (0, j)),
                  pl.BlockSpec((kr, tn), lambda j, i: (1, j)),
                  pl.BlockSpec((tm, tn), lambda j, i: (i, j)),
                  pl.BlockSpec((1, 1, tn), lambda j, i: (_mod_idx(layer, _seg(i, nct, lt), 2), 0, j))],
        out_specs=pl.BlockSpec((tm, tn), lambda j, i: (i, j)),
        out_shape=jax.ShapeDtypeStruct((t, d), F32),
        compiler_params=_cp(("parallel", "parallel"), 48),
        name="wout_residual",
    )(mix_c, mix_r, w_out, w_out, xs, mods)


def _router(xs, g, mods, layer, rw, rb, dims):
    t, d = xs.shape
    half = d // 2
    tl, nct, lt = dims["TL"], dims["NCT"], dims["LT"]

    def body(x_ref, g_ref, sh_ref, sc_ref, rw_ref, rb_ref, hp_ref, e4_ref, p4_ref, mask_ref, cnt_ref):
        x = x_ref[...]
        ms = jnp.mean(x * x, axis=-1, keepdims=True)
        h = (x * lax.rsqrt(ms + EPS) * g_ref[...]) * (1.0 + sc_ref[0]) + sh_ref[0]
        hh = h.astype(BF16)
        hh32 = hh.astype(F32)
        bits = pltpu.bitcast(hh32, U32)
        hp_ref[...] = (bits[:, half:] & jnp.uint32(0xFFFF0000)) | lax.shift_right_logical(bits[:, :half], jnp.uint32(16))

        w = rw_ref[...]
        wh = w.astype(BF16)
        wl = (w - wh.astype(F32)).astype(BF16)
        hl = (h - hh32).astype(BF16)
        logits = _dot(hh, wh) + (_dot(hh, wl) + _dot(hl, wh)) + rb_ref[...]

        lane = lax.broadcasted_iota(I32, (tl, LANES), 1)
        vals = logits
        tops, hots = [], []
        e4 = jnp.zeros((tl, LANES), I32)
        for k in range(TOP_K):
            m = jnp.max(vals, axis=-1, keepdims=True)
            sel = jnp.min(jnp.where(vals == m, lane, LANES), axis=-1, keepdims=True)
            hot = lane == sel
            tops.append(m)
            hots.append(hot)
            vals = jnp.where(hot, -jnp.inf, vals)
            e4 = jnp.where(lane == k, sel, e4)
        ex = [jnp.exp(m - tops[0]) for m in tops]
        den = ex[0]
        for e in ex[1:]:
            den = den + e
        p4 = jnp.zeros((tl, LANES), F32)
        mask = jnp.zeros((tl, LANES), F32)
        for k in range(TOP_K):
            p4 = jnp.where(lane == k, ex[k] / den, p4)
            mask = jnp.where(hots[k], 1.0, mask)
        e4_ref[...] = e4
        p4_ref[...] = p4
        mask_ref[...] = mask

        @pl.when(pl.program_id(0) == 0)
        def _():
            cnt_ref[...] = jnp.zeros_like(cnt_ref)

        cnt_ref[...] = cnt_ref[...] + jnp.sum(mask, axis=0, keepdims=True)

    lanes_out = pl.BlockSpec((tl, LANES), lambda i: (i, 0))
    return pl.pallas_call(
        body,
        grid=(t // tl,),
        in_specs=[pl.BlockSpec((tl, d), lambda i: (i, 0)),
                  pl.BlockSpec((1, d), lambda i: (0, 0)),
                  pl.BlockSpec((1, 1, d), lambda i: (_mod_idx(layer, _seg(i, nct, lt), 3), 0, 0)),
                  pl.BlockSpec((1, 1, d), lambda i: (_mod_idx(layer, _seg(i, nct, lt), 4), 0, 0)),
                  pl.BlockSpec((d, LANES), lambda i: (0, 0)),
                  pl.BlockSpec((1, LANES), lambda i: (0, 0))],
        out_specs=[pl.BlockSpec((tl, half), lambda i: (i, 0)), lanes_out, lanes_out, lanes_out,
                   pl.BlockSpec((SUBLANES, LANES), lambda i: (0, 0))],
        out_shape=[jax.ShapeDtypeStruct((t, half), U32),
                   jax.ShapeDtypeStruct((t, LANES), I32),
                   jax.ShapeDtypeStruct((t, LANES), F32),
                   jax.ShapeDtypeStruct((t, LANES), F32),
                   jax.ShapeDtypeStruct((SUBLANES, LANES), F32)],
        compiler_params=_cp(("arbitrary",), 40),
        name="router",
    )(xs, g.reshape(1, d), mods, mods, rw, rb)


def _positions(mask, e4, cnt, n_experts, n_tiles, dims):
    t = mask.shape[0]
    tl = dims["TL"]
    te = EXPERT_TILE
    ntp = -(-n_tiles // SUBLANES) * SUBLANES

    def lane_cumsum(x, lane):
        sh = 1
        while sh < LANES:
            x = x + jnp.where(lane >= sh, pltpu.roll(x, sh, axis=1), 0.0)
            sh *= 2
        return x

    def body(mask_ref, e4_ref, cnt_ref, pos_ref, tab_ref, carry):
        i = pl.program_id(0)
        lane8 = lax.broadcasted_iota(I32, (SUBLANES, LANES), 1)
        cnt = cnt_ref[...]
        gp = jnp.ceil(cnt * (1.0 / te)) * te
        inc = lane_cumsum(gp, lane8)
        off = inc - gp

        @pl.when(i == 0)
        def _():
            carry[...] = jnp.zeros_like(carry)
            lane = lax.broadcasted_iota(I32, (ntp, LANES), 1)
            tile = lax.broadcasted_iota(I32, (ntp, LANES), 0).astype(F32)
            total = jnp.max(inc[0:1], axis=-1, keepdims=True)
            start = tile * te
            used = start < total
            start_c = jnp.minimum(start, total - te)
            ex = jnp.sum(jnp.where(inc[0:1] <= start_c, 1.0, 0.0), axis=-1, keepdims=True)
            ex = jnp.minimum(ex, n_experts - 1.0)
            hot = lane.astype(F32) == ex
            cnt_e = jnp.sum(jnp.where(hot, cnt[0:1], 0.0), axis=-1, keepdims=True)
            off_e = jnp.sum(jnp.where(hot, off[0:1], 0.0), axis=-1, keepdims=True)
            valid = jnp.where(used, jnp.clip(cnt_e - (start - off_e), 0.0, te), 0.0)
            tab = jnp.where(lane == 0, ex, jnp.where(lane == 1, valid, 0.0))
            tab_ref[...] = tab.astype(I32)

        m = mask_ref[...]
        r = lax.broadcasted_iota(I32, (tl, tl), 0)
        c = lax.broadcasted_iota(I32, (tl, tl), 1)
        tri = jnp.where(r > c, 1.0, 0.0).astype(BF16)
        rank = _dot(tri, m.astype(BF16))
        posf = off[0:1] + carry[0:1] + rank
        lane = lax.broadcasted_iota(I32, (tl, LANES), 1)
        e4 = e4_ref[...]
        pos4 = jnp.zeros((tl, LANES), F32)
        for k in range(TOP_K):
            ek = jnp.sum(jnp.where(lane == k, e4, 0), axis=-1, keepdims=True)
            pk = jnp.sum(jnp.where(lane == ek, posf, 0.0), axis=-1, keepdims=True)
            pos4 = jnp.where(lane == k, pk, pos4)
        pos_ref[...] = pos4.astype(I32)
        carry[...] = carry[...] + jnp.sum(m, axis=0, keepdims=True)

    rows = pl.BlockSpec((tl, LANES), lambda i: (i, 0))
    return pl.pallas_call(
        body,
        grid=(t // tl,),
        in_specs=[rows, rows, pl.BlockSpec((SUBLANES, LANES), lambda i: (0, 0))],
        out_specs=[rows, pl.BlockSpec((ntp, LANES), lambda i: (0, 0))],
        out_shape=[jax.ShapeDtypeStruct((t, LANES), I32), jax.ShapeDtypeStruct((ntp, LANES), I32)],
        scratch_shapes=[pltpu.VMEM((SUBLANES, LANES), F32)],
        compiler_params=_cp(("arbitrary",), 32),
        name="moe_positions",
    )(mask, e4, cnt)


def _invert(pos_flat, n_slots):
    n = pos_flat.shape[0]

    def body(pos_ref, inv_ref):
        def zero(s, c):
            inv_ref[s] = 0
            return c

        lax.fori_loop(0, n_slots, zero, 0, unroll=8)

        def put(q, c):
            inv_ref[pos_ref[q]] = lax.shift_right_logical(q, 2)
            return c

        lax.fori_loop(0, n, put, 0, unroll=8)

    assert TOP_K == 4
    return pl.pallas_call(
        body,
        in_specs=[pl.BlockSpec(memory_space=pltpu.SMEM)],
        out_specs=pl.BlockSpec(memory_space=pltpu.SMEM),
        out_shape=jax.ShapeDtypeStruct((n_slots,), I32),
        name="moe_invert",
    )(pos_flat)


def _experts(hp, inv, tab_e, tab_v, wgu, bgu, wdn, bdn, n_tiles):
    t, half = hp.shape
    d = 2 * half
    n_exp, _, de2 = wgu.shape
    de = de2 // 2
    te = EXPERT_TILE
    inv3 = inv.reshape(n_tiles, 1, te)

    def body(te_ref, tv_ref, cur_ref, nxt_ref, hp_ref, wgu_ref, bgu_ref, wdn_ref, bdn_ref,
             ys_ref, xbuf, sem):
        i = pl.program_id(0)
        slot = i % 2

        def row_copy(tok, sl, r):
            return pltpu.make_async_copy(hp_ref.at[pl.ds(tok, 1)], xbuf.at[sl, pl.ds(r, 1)], sem.at[sl])

        def issue(idx_ref, sl):
            def f(r, c):
                row_copy(idx_ref[0, 0, r], sl, r).start()
                return c

            lax.fori_loop(0, te, f, 0, unroll=8)

        @pl.when(jnp.logical_and(i == 0, tv_ref[0] > 0))
        def _():
            issue(cur_ref, 0)

        nxt = jnp.minimum(i + 1, n_tiles - 1)

        @pl.when(jnp.logical_and(i + 1 < n_tiles, tv_ref[nxt] > 0))
        def _():
            issue(nxt_ref, 1 - slot)

        @pl.when(tv_ref[i] > 0)
        def _():
            def w(r, c):
                row_copy(0, slot, r).wait()
                return c

            lax.fori_loop(0, te, w, 0, unroll=8)
            words = xbuf[slot]
            lo = pltpu.bitcast(lax.shift_left(words, jnp.uint32(16)), F32).astype(BF16)
            hi = pltpu.bitcast(words & jnp.uint32(0xFFFF0000), F32).astype(BF16)
            gu = _dot(lo, wgu_ref[0, :half, :]) + _dot(hi, wgu_ref[0, half:, :]) + bgu_ref[0]
            gate = jnp.minimum(gu[:, :de], SWIGLU_LIMIT)
            up = jnp.clip(gu[:, de:], -SWIGLU_LIMIT, SWIGLU_LIMIT)
            act = (up + 1.0) * gate * jax.nn.sigmoid(SWIGLU_ALPHA * gate)
            ys_ref[...] = _dot(act.astype(BF16), wdn_ref[0]) + bdn_ref[0]

        @pl.when(tv_ref[i] == 0)
        def _():
            ys_ref[...] = jnp.zeros_like(ys_ref)

    grid_spec = pltpu.PrefetchScalarGridSpec(
        num_scalar_prefetch=2,
        grid=(n_tiles,),
        in_specs=[pl.BlockSpec((1, 1, te), lambda i, e, v: (i, 0, 0), memory_space=pltpu.SMEM),
                  pl.BlockSpec((1, 1, te), lambda i, e, v: (jnp.minimum(i + 1, n_tiles - 1), 0, 0),
                               memory_space=pltpu.SMEM),
                  pl.BlockSpec(memory_space=pl.ANY),
                  pl.BlockSpec((1, d, de2), lambda i, e, v: (e[i], 0, 0)),
                  pl.BlockSpec((1, 1, de2), lambda i, e, v: (e[i], 0, 0)),
                  pl.BlockSpec((1, de, d), lambda i, e, v: (e[i], 0, 0)),
                  pl.BlockSpec((1, 1, d), lambda i, e, v: (e[i], 0, 0))],
        out_specs=pl.BlockSpec((te, d), lambda i, e, v: (i, 0)),
        scratch_shapes=[pltpu.VMEM((2, te, half), U32), pltpu.SemaphoreType.DMA((2,))],
    )
    return pl.pallas_call(
        body,
        grid_spec=grid_spec,
        out_shape=jax.ShapeDtypeStruct((n_tiles * te, d), F32),
        compiler_params=_cp(("arbitrary",), 56),
        name="moe_experts",
    )(tab_e, tab_v, inv3, inv3, hp, wgu, bgu.reshape(n_exp, 1, de2), wdn, bdn.reshape(n_exp, 1, d))


def _combine(xs, ys, pos_flat, p4, mods, layer, dims, final_g):
    t, d = xs.shape
    tc = COMBINE_TILE
    nq = tc * TOP_K
    nct_rows = dims["B"] * dims["CTX"]
    nct, lt = nct_rows // tc, dims["L"] // tc
    final = final_g is not None
    skip = nct if final else 0
    n_steps = t // tc - skip
    pos3 = pos_flat.reshape(t // tc, 1, nq)

    def body(cur_ref, nxt_ref, x_ref, p_ref, g_ref, fg_ref, ys_ref, o_ref, gbuf, sem):
        i = pl.program_id(0)
        slot = i % 2

        def row_copy(src, sl, q):
            return pltpu.make_async_copy(ys_ref.at[pl.ds(src, 1)],
                                         gbuf.at[sl, q % TOP_K, pl.ds(q // TOP_K, 1)], sem.at[sl])

        def issue(idx_ref, sl):
            def f(q, c):
                row_copy(idx_ref[0, 0, q], sl, q).start()
                return c

            lax.fori_loop(0, nq, f, 0, unroll=8)

        @pl.when(i == 0)
        def _():
            issue(cur_ref, 0)

        @pl.when(i + 1 < n_steps)
        def _():
            issue(nxt_ref, 1 - slot)

        def w(q, c):
            row_copy(0, slot, q).wait()
            return c

        lax.fori_loop(0, nq, w, 0, unroll=8)
        p = p_ref[...]
        y = p[:, 0:1] * gbuf[slot, 0]
        for k in range(1, TOP_K):
            y = y + p[:, k:k + 1] * gbuf[slot, k]
        xn = x_ref[...] + g_ref[0] * y
        if final:
            xn = xn * lax.rsqrt(jnp.mean(xn * xn, axis=-1, keepdims=True) + EPS) * fg_ref[...]
        o_ref[...] = xn

    fg = jnp.ones((1, d), F32) if final_g is None else final_g.reshape(1, d)
    return pl.pallas_call(
        body,
        grid=(n_steps,),
        in_specs=[pl.BlockSpec((1, 1, nq), lambda i: (i + skip, 0, 0), memory_space=pltpu.SMEM),
                  pl.BlockSpec((1, 1, nq), lambda i: (jnp.minimum(i + 1, n_steps - 1) + skip, 0, 0),
                               memory_space=pltpu.SMEM),
                  pl.BlockSpec((tc, d), lambda i: (i + skip, 0)),
                  pl.BlockSpec((tc, LANES), lambda i: (i + skip, 0)),
                  pl.BlockSpec((1, 1, d), lambda i: (_mod_idx(layer, _seg(i + skip, nct, lt), 5), 0, 0)),
                  pl.BlockSpec((1, d), lambda i: (0, 0)),
                  pl.BlockSpec(memory_space=pl.ANY)],
        out_specs=pl.BlockSpec((tc, d), lambda i: (i, 0)),
        out_shape=jax.ShapeDtypeStruct((n_steps * tc, d), F32),
        scratch_shapes=[pltpu.VMEM((2, TOP_K, tc, d), F32), pltpu.SemaphoreType.DMA((2,))],
        compiler_params=_cp(("arbitrary",), 40),
        name="moe_combine",
    )(pos3, pos3, xs, p4, mods, fg, ys)


def kernel(x, c, ctx, c_ctx, norm1_g, norm2_g, w_ada, b_ada, w_in, conv_w, conv_b, gn_g, gn_b, lru_conv_w, lru_conv_b, lru_w_a, lru_b_a, lru_w_x, lru_b_x, lru_lambda, mix_norm_g, w_out, router_w, router_b, exp_w_gu, exp_b_gu, exp_w_down, exp_b_down, final_norm_g):
    nb, seq, d = x.shape
    ctx_len = ctx.shape[1]
    depth = w_in.shape[0]
    cw = conv_w.shape[2]
    n_exp = router_w.shape[2]
    de = exp_w_down.shape[2]
    tl = ctx_len
    t = nb * (ctx_len + seq)
    assert nb + 1 <= MOD_ROWS and seq % tl == 0 and n_exp <= LANES
    tm = min(512, nb * ctx_len)
    assert (nb * ctx_len) % tm == 0 and seq % tm == 0
    assert (nb * ctx_len) % COMBINE_TILE == 0 and seq % COMBINE_TILE == 0
    dims = {"B": nb, "CTX": ctx_len, "L": seq, "TL": tl, "NCT": nb, "LT": seq // tl, "TM": tm}
    n_tiles = (t * TOP_K) // EXPERT_TILE + n_exp
    n_slots = n_tiles * EXPERT_TILE

    cvec = jnp.concatenate([c_ctx[None], c, jnp.zeros((MOD_ROWS - 1 - nb, d), F32)], axis=0)
    mods = _adaln(cvec, w_ada, b_ada).reshape(depth * MOD_ROWS * N_MOD, 1, d)
    xs = jnp.concatenate([ctx.reshape(nb * ctx_len, d), x.reshape(nb * seq, d)], axis=0)

    for l in range(depth):
        last = l == depth - 1
        hl = _prenorm(xs, norm1_g[l], mods, l, dims)
        zl = _matmul(hl, w_in[l].astype(BF16), tm, min(1024, w_in.shape[2]), "in_proj")
        mix_c = _conv_module(zl, conv_w[l], conv_b[l], gn_g[l], gn_b[l], mix_norm_g[l, :cw], dims)
        af, bf, ab, bb = _lru_gates(zl, lru_conv_w[l], lru_conv_b[l], lru_w_a[l], lru_b_a[l],
                                    lru_w_x[l], lru_b_x[l], lru_lambda[l], dims)
        hf, hb = _scan(af, bf, ab, bb, dims)
        mix_r = _lru_out(zl, hf, hb, mix_norm_g[l, cw:], dims)
        xs = _wout_residual(mix_c, mix_r, w_out[l].astype(BF16), xs, mods, l, dims)

        rw = jnp.pad(router_w[l], ((0, 0), (0, LANES - n_exp)))
        rb = jnp.pad(router_b[l], (0, LANES - n_exp), constant_values=NEG_BIG).reshape(1, LANES)
        hp, e4, p4, mask, cnt = _router(xs, norm2_g[l], mods, l, rw, rb, dims)
        pos4, tab = _positions(mask, e4, cnt, n_exp, n_tiles, dims)
        pos_flat = pos4[:, :TOP_K].reshape(t * TOP_K)
        inv = _invert(pos_flat, n_slots)
        wgu = jnp.concatenate([exp_w_gu[l][:, :, 0::2], exp_w_gu[l][:, :, 1::2]], axis=-1).astype(BF16)
        bgu = jnp.concatenate([exp_b_gu[l][:, 0::2], exp_b_gu[l][:, 1::2]], axis=-1)
        ys = _experts(hp, inv, tab[:n_tiles, 0], tab[:n_tiles, 1],
                      wgu, bgu, exp_w_down[l].astype(BF16), exp_b_down[l], n_tiles)
        xs = _combine(xs, ys, pos_flat, p4, mods, l, dims, final_norm_g if last else None)

    return xs.reshape(nb, seq, d)
```

```python
import jax
import jax.numpy as jnp
from jax import lax
from jax.experimental import pallas as pl
from jax.experimental.pallas import tpu as pltpu

GRID_W = 64
CONV_GROUPS = 16
TOP_K = 4
N_MOD = 6
EPS = 1e-6
LRU_C = 8.0
SWIGLU_LIMIT = 7.0
SWIGLU_ALPHA = 1.702

LANES = 128
SUBLANES = 8
MOD_ROWS = 8
HALO = 16
CONV_ROWS = 64
EXPERT_TILE = 256
COMBINE_TILE = 64
NEG_BIG = -1e30

F32 = jnp.float32
BF16 = jnp.bfloat16
U32 = jnp.uint32
I32 = jnp.int32


def _cp(sems, vmem_mb):
    return pltpu.CompilerParams(dimension_semantics=sems, vmem_limit_bytes=vmem_mb << 20)


def _seg(i, n_ctx_tiles, lat_tiles):
    return jnp.where(i < n_ctx_tiles, 0, 1 + (i - n_ctx_tiles) // lat_tiles)


def _mod_idx(layer, seg, j):
    return (layer * MOD_ROWS + seg) * N_MOD + j


def _dot(a, b):
    return jnp.dot(a, b, preferred_element_type=F32)


def _token_rows(width):
    rows = width // LANES
    return rows, rows + 4


def _adaln(cvec, w_ada, b_ada):
    depth, d, n = w_ada.shape
    tn = min(n, 1024)

    def body(c_ref, w_ref, b_ref, o_ref):
        c = c_ref[...]
        s = (c * jax.nn.sigmoid(c)).astype(BF16)
        o_ref[0] = _dot(s, w_ref[0].astype(BF16)) + b_ref[0]

    return pl.pallas_call(
        body,
        grid=(depth, n // tn),
        in_specs=[pl.BlockSpec((MOD_ROWS, d), lambda l, j: (0, 0)),
                  pl.BlockSpec((1, d, tn), lambda l, j: (l, 0, j)),
                  pl.BlockSpec((1, 1, tn), lambda l, j: (l, 0, j))],
        out_specs=pl.BlockSpec((1, MOD_ROWS, tn), lambda l, j: (l, 0, j)),
        out_shape=jax.ShapeDtypeStruct((depth, MOD_ROWS, n), F32),
        compiler_params=_cp(("parallel", "parallel"), 48),
        name="adaln",
    )(cvec, w_ada, b_ada.reshape(depth, 1, n))


def _prenorm(xs, g, mods, layer, dims):
    t, d = xs.shape
    tl, nct, lt = dims["TL"], dims["NCT"], dims["LT"]

    def body(x_ref, g_ref, sh_ref, sc_ref, o_ref):
        x = x_ref[...]
        ms = jnp.mean(x * x, axis=-1, keepdims=True)
        y = x * lax.rsqrt(ms + EPS) * g_ref[...]
        o_ref[...] = (y * (1.0 + sc_ref[0]) + sh_ref[0]).astype(o_ref.dtype)

    return pl.pallas_call(
        body,
        grid=(t // tl,),
        in_specs=[pl.BlockSpec((tl, d), lambda i: (i, 0)),
                  pl.BlockSpec((1, d), lambda i: (0, 0)),
                  pl.BlockSpec((1, 1, d), lambda i: (_mod_idx(layer, _seg(i, nct, lt), 0), 0, 0)),
                  pl.BlockSpec((1, 1, d), lambda i: (_mod_idx(layer, _seg(i, nct, lt), 1), 0, 0))],
        out_specs=pl.BlockSpec((tl, d), lambda i: (i, 0)),
        out_shape=jax.ShapeDtypeStruct((t, d), BF16),
        compiler_params=_cp(("parallel",), 32),
        name="prenorm1",
    )(xs, g.reshape(1, d), mods, mods)


def _matmul(a, w, tm, tn, name):
    m, k = a.shape
    n = w.shape[1]

    def body(a_ref, w_ref, o_ref, wb):
        @pl.when(pl.program_id(1) == 0)
        def _():
            wb[...] = w_ref[...].astype(BF16)

        o_ref[...] = _dot(a_ref[...], wb[...])

    return pl.pallas_call(
        body,
        grid=(n // tn, m // tm),
        in_specs=[pl.BlockSpec((tm, k), lambda j, i: (i, 0)),
                  pl.BlockSpec((k, tn), lambda j, i: (0, j))],
        out_specs=pl.BlockSpec((tm, tn), lambda j, i: (i, j)),
        out_shape=jax.ShapeDtypeStruct((m, n), F32),
        scratch_shapes=[pltpu.VMEM((k, tn), BF16)],
        compiler_params=_cp(("parallel", "arbitrary"), 48),
        name=name,
    )(a, w)


def _conv_module(zl, conv_w, conv_b, gn_g, gn_b, mix_g, dims):
    t = zl.shape[0]
    k_taps, cw = conv_w.shape
    pad = (k_taps - 1) // 2
    tl, nct, ctx_len = dims["TL"], dims["NCT"], dims["CTX"]
    assert cw // CONV_GROUPS == LANES and pad <= HALO
    assert tl % GRID_W == 0 and GRID_W % CONV_ROWS == 0 and ctx_len % CONV_ROWS == 0
    upad_rows = max((tl // GRID_W) * (GRID_W + 2 * HALO), ctx_len + 2 * HALO)

    def body(za_ref, zg_ref, w_ref, b_ref, gg_ref, gb_ref, mg_ref, o_ref, upad, ybuf, ssbuf):
        i = pl.program_id(0)
        upad[...] = jnp.zeros_like(upad)
        ssbuf[...] = jnp.zeros_like(ssbuf)

        def path(seg_len):
            nseg = tl // seg_len
            stride = seg_len + 2 * HALO

            def chunk(c, carry):
                ln = pl.ds(pl.multiple_of(c * LANES, LANES), LANES)
                u = za_ref[:, ln] * jax.nn.sigmoid(zg_ref[:, ln])
                for s in range(nseg):
                    upad[s * stride + HALO:s * stride + HALO + seg_len, :] = u[s * seg_len:(s + 1) * seg_len]
                bias = b_ref[:, ln]
                gg = gg_ref[:, ln]
                gb = gb_ref[:, ln]
                for s in range(nseg):
                    for r0 in range(0, seg_len, CONV_ROWS):
                        base = s * stride + HALO + r0 - pad
                        acc = jnp.broadcast_to(bias, (CONV_ROWS, LANES))
                        for k in range(k_taps):
                            acc = acc + w_ref[pl.ds(k, 1), ln] * upad[base + k:base + k + CONV_ROWS, :]
                        mu = jnp.mean(acc, axis=-1, keepdims=True)
                        dv = acc - mu
                        var = jnp.mean(dv * dv, axis=-1, keepdims=True)
                        y = dv * lax.rsqrt(var + EPS) * gg + gb
                        y = y * jax.nn.sigmoid(y)
                        rows = pl.ds(s * seg_len + r0, CONV_ROWS)
                        ybuf[rows, ln] = y
                        ssbuf[rows, :] = ssbuf[rows, :] + jnp.sum(y * y, axis=-1, keepdims=True)
                return carry

            lax.fori_loop(0, cw // LANES, chunk, 0)

        @pl.when(i < nct)
        def _():
            path(ctx_len)

        @pl.when(i >= nct)
        def _():
            path(GRID_W)

        inv = lax.rsqrt(ssbuf[:, 0:1] * (1.0 / cw) + EPS)
        o_ref[...] = (ybuf[...] * inv * mg_ref[...]).astype(o_ref.dtype)

    vec = pl.BlockSpec((1, cw), lambda i: (0, 0))
    return pl.pallas_call(
        body,
        grid=(t // tl,),
        in_specs=[pl.BlockSpec((tl, cw), lambda i: (i, 0)),
                  pl.BlockSpec((tl, cw), lambda i: (i, 1)),
                  pl.BlockSpec((k_taps, cw), lambda i: (0, 0)),
                  vec, vec, vec, vec],
        out_specs=pl.BlockSpec((tl, cw), lambda i: (i, 0)),
        out_shape=jax.ShapeDtypeStruct((t, cw), BF16),
        scratch_shapes=[pltpu.VMEM((upad_rows, LANES), F32),
                        pltpu.VMEM((tl, cw), F32),
                        pltpu.VMEM((tl, LANES), F32)],
        compiler_params=_cp(("parallel",), 40),
        name="conv_module",
    )(zl, zl, conv_w, conv_b.reshape(1, cw), gn_g.reshape(1, cw), gn_b.reshape(1, cw), mix_g.reshape(1, cw))


def _lru_gates(zl, cw, cb, wa, ba, wx, bx, lam, dims):
    t = zl.shape[0]
    ndir, kc, lw = cw.shape
    heads, hd = wa.shape[1], wa.shape[2]
    tl, nct, lt = dims["TL"], dims["NCT"], dims["LT"]
    assert hd == LANES and heads * hd == lw and ndir == 2 and kc - 1 <= SUBLANES
    nrb = t // SUBLANES
    tls = tl // SUBLANES

    def body(x_ref, p_ref, n_ref, cw_ref, cb_ref, wa_ref, ba_ref, wx_ref, bx_ref, lam_ref,
             af_ref, bf_ref, ab_ref, bb_ref, xp):
        i = pl.program_id(0)
        is_ctx = i < nct
        j = jnp.where(is_ctx, 0, (i - nct) % lt)
        first = jnp.logical_or(is_ctx, j == 0)
        last = jnp.logical_or(is_ctx, j == lt - 1)
        outs = ((af_ref, bf_ref), (ab_ref, bb_ref))

        def head(h, carry):
            ln = pl.ds(pl.multiple_of(h * LANES, LANES), LANES)
            xp[0:SUBLANES, :] = jnp.where(first, 0.0, p_ref[:, ln])
            xp[SUBLANES:SUBLANES + tl, :] = x_ref[:, ln]
            xp[SUBLANES + tl:2 * SUBLANES + tl, :] = jnp.where(last, 0.0, n_ref[:, ln])
            win = {o: xp[SUBLANES + o:SUBLANES + o + tl, :] for o in range(-(kc - 1), kc)}
            for d in range(ndir):
                xc = jnp.broadcast_to(cb_ref[pl.ds(d, 1), ln], (tl, LANES))
                for k in range(kc):
                    off = k - (kc - 1) if d == 0 else (kc - 1) - k
                    xc = xc + cw_ref[d, pl.ds(k, 1), ln] * win[off]
                xcb = xc.astype(BF16)
                r = jax.nn.sigmoid(_dot(xcb, wa_ref[d, h].astype(BF16)) + ba_ref[pl.ds(d, 1), ln])
                ig = jax.nn.sigmoid(_dot(xcb, wx_ref[d, h].astype(BF16)) + bx_ref[pl.ds(d, 1), ln])
                sp = jax.nn.softplus(-lam_ref[pl.ds(d, 1), ln])
                log_a = (-LRU_C * sp) * r
                a_ref, b_ref = outs[d]
                a = jnp.exp(log_a)
                a_ref[:, ln] = a
                b_ref[:, ln] = jnp.sqrt(1.0 - a * a) * (ig * xc)
            return carry

        lax.fori_loop(0, heads, head, 0)

    xcol = 2
    full2 = pl.BlockSpec((ndir, lw), lambda i: (0, 0))
    full4 = pl.BlockSpec((ndir, heads, hd, hd), lambda i: (0, 0, 0, 0))
    out = pl.BlockSpec((tl, lw), lambda i: (i, 0))
    shp = jax.ShapeDtypeStruct((t, lw), F32)
    return pl.pallas_call(
        body,
        grid=(t // tl,),
        in_specs=[pl.BlockSpec((tl, lw), lambda i: (i, xcol)),
                  pl.BlockSpec((SUBLANES, lw), lambda i: (jnp.maximum(i * tls - 1, 0), xcol)),
                  pl.BlockSpec((SUBLANES, lw), lambda i: (jnp.minimum((i + 1) * tls, nrb - 1), xcol)),
                  pl.BlockSpec((ndir, kc, lw), lambda i: (0, 0, 0)),
                  full2, full4, full2, full4, full2, full2],
        out_specs=[out, out, out, out],
        out_shape=[shp, shp, shp, shp],
        scratch_shapes=[pltpu.VMEM((tl + 2 * SUBLANES, LANES), F32)],
        compiler_params=_cp(("parallel",), 40),
        name="lru_gates",
    )(zl, zl, zl, cw, cb, wa, ba, wx, bx, lam)


def _scan(af, bf, ab, bb, dims):
    t, lw = af.shape
    hs = lw // LANES
    tl, nct, lt, nb = dims["TL"], dims["NCT"], dims["LT"], dims["B"]
    assert nct == nb

    def v(z):
        return z.reshape(t, hs, LANES)

    def fidx(b, s):
        return jnp.where(s == 0, b, nct + b * lt + s - 1)

    def bidx(b, s):
        return jnp.where(s == 0, b, nct + b * lt + lt - s)

    def body(af_ref, bf_ref, ab_ref, bb_ref, hf_ref, hb_ref, hc):
        @pl.when(pl.program_id(1) == 0)
        def _():
            hc[...] = jnp.zeros_like(hc)

        def step(tt, carry):
            hf, hb = carry
            tb = tl - 1 - tt
            hf = af_ref[tt] * hf + bf_ref[tt]
            hf_ref[tt] = hf
            hb = ab_ref[tb] * hb + bb_ref[tb]
            hb_ref[tb] = hb
            return hf, hb

        hf, hb = lax.fori_loop(0, tl, step, (hc[0], hc[1]), unroll=8)
        hc[0] = hf
        hc[1] = hb

    fspec = pl.BlockSpec((tl, hs, LANES), lambda b, s: (fidx(b, s), 0, 0))
    bspec = pl.BlockSpec((tl, hs, LANES), lambda b, s: (bidx(b, s), 0, 0))
    shp = jax.ShapeDtypeStruct((t, hs, LANES), F32)
    hf, hb = pl.pallas_call(
        body,
        grid=(nb, 1 + lt),
        in_specs=[fspec, fspec, bspec, bspec],
        out_specs=[fspec, bspec],
        out_shape=[shp, shp],
        scratch_shapes=[pltpu.VMEM((2, hs, LANES), F32)],
        compiler_params=_cp(("parallel", "arbitrary"), 40),
        name="lru_scan",
    )(v(af), v(bf), v(ab), v(bb))
    return hf.reshape(t, lw), hb.reshape(t, lw)


def _lru_out(zl, hf, hb, mix_g, dims):
    t, lw = hf.shape
    tl = dims["TL"]
    gcol = 3

    def body(g_ref, hf_ref, hb_ref, mg_ref, o_ref):
        y = jax.nn.gelu(g_ref[...], approximate=True) * (hf_ref[...] + hb_ref[...])
        inv = lax.rsqrt(jnp.mean(y * y, axis=-1, keepdims=True) + EPS)
        o_ref[...] = (y * inv * mg_ref[...]).astype(o_ref.dtype)

    row = pl.BlockSpec((tl, lw), lambda i: (i, 0))
    return pl.pallas_call(
        body,
        grid=(t // tl,),
        in_specs=[pl.BlockSpec((tl, lw), lambda i: (i, gcol)), row, row,
                  pl.BlockSpec((1, lw), lambda i: (0, 0))],
        out_specs=row,
        out_shape=jax.ShapeDtypeStruct((t, lw), BF16),
        compiler_params=_cp(("parallel",), 40),
        name="lru_out",
    )(zl, hf, hb, mix_g.reshape(1, lw))


def _wout_residual(mix_c, mix_r, w_out, xs, mods, layer, dims):
    t, d = xs.shape
    kc = mix_c.shape[1]
    kr = mix_r.shape[1]
    assert kc == kr
    tm, tn = dims["TM"], min(d, 512)
    nct, lt = dims["B"] * dims["CTX"] // tm, dims["L"] // tm

    def body(a1_ref, a2_ref, w1_ref, w2_ref, x_ref, g_ref, o_ref, wb1, wb2):
        @pl.when(pl.program_id(1) == 0)
        def _():
            wb1[...] = w1_ref[...].astype(BF16)
            wb2[...] = w2_ref[...].astype(BF16)

        acc = _dot(a1_ref[...], wb1[...]) + _dot(a2_ref[...], wb2[...])
        o_ref[...] = x_ref[...] + g_ref[0] * acc

    return pl.pallas_call(
        body,
        grid=(d // tn, t // tm),
        in_specs=[pl.BlockSpec((tm, kc), lambda j, i: (i, 0)),
                  pl.BlockSpec((tm, kr), lambda j, i: (i, 0)),
                  pl.BlockSpec((kc, tn), lambda j, i: (0, j)),
                  pl.BlockSpec((kr, tn), lambda j, i: (1, j)),
                  pl.BlockSpec((tm, tn), lambda j, i: (i, j)),
                  pl.BlockSpec((1, 1, tn), lambda j, i: (_mod_idx(layer, _seg(i, nct, lt), 2), 0, j))],
        out_specs=pl.BlockSpec((tm, tn), lambda j, i: (i, j)),
        out_shape=jax.ShapeDtypeStruct((t, d), F32),
        scratch_shapes=[pltpu.VMEM((kc, tn), BF16), pltpu.VMEM((kr, tn), BF16)],
        compiler_params=_cp(("parallel", "arbitrary"), 48),
        name="wout_residual",
    )(mix_c, mix_r, w_out, w_out, xs, mods)


def _router(xs, g, mods, layer, rw, rb, dims):
    t, d = xs.shape
    half = d // 2
    rows_h, pitch = _token_rows(half)
    tl, nct, lt = dims["TL"], dims["NCT"], dims["LT"]

    def body(x_ref, g_ref, sh_ref, sc_ref, rw_ref, rb_ref, hp_ref, e4_ref, p4_ref, mask_ref, cnt_ref):
        x = x_ref[...]
        ms = jnp.mean(x * x, axis=-1, keepdims=True)
        h = (x * lax.rsqrt(ms + EPS) * g_ref[...]) * (1.0 + sc_ref[0]) + sh_ref[0]
        hh = h.astype(BF16)
        hh32 = hh.astype(F32)
        bits = pltpu.bitcast(hh32, U32)
        words = (bits[:, half:] & jnp.uint32(0xFFFF0000)) | lax.shift_right_logical(bits[:, :half], jnp.uint32(16))
        for cidx in range(pitch):
            chunk = words[:, cidx * LANES:(cidx + 1) * LANES] if cidx < rows_h else jnp.zeros((tl, LANES), U32)
            hp_ref[pl.ds(cidx, tl, stride=pitch), :] = chunk

        w = rw_ref[...]
        wh = w.astype(BF16)
        wl = (w - wh.astype(F32)).astype(BF16)
        hl = (h - hh32).astype(BF16)
        logits = _dot(hh, wh) + (_dot(hh, wl) + _dot(hl, wh)) + rb_ref[...]

        lane = lax.broadcasted_iota(I32, (tl, LANES), 1)
        vals = logits
        tops, hots = [], []
        e4 = jnp.zeros((tl, LANES), I32)
        for k in range(TOP_K):
            m = jnp.max(vals, axis=-1, keepdims=True)
            sel = jnp.min(jnp.where(vals == m, lane, LANES), axis=-1, keepdims=True)
            hot = lane == sel
            tops.append(m)
            hots.append(hot)
            vals = jnp.where(hot, -jnp.inf, vals)
            e4 = jnp.where(lane == k, sel, e4)
        ex = [jnp.exp(m - tops[0]) for m in tops]
        den = ex[0]
        for e in ex[1:]:
            den = den + e
        p4 = jnp.zeros((tl, LANES), F32)
        mask = jnp.zeros((tl, LANES), F32)
        for k in range(TOP_K):
            p4 = jnp.where(lane == k, ex[k] / den, p4)
            mask = jnp.where(hots[k], 1.0, mask)
        e4_ref[...] = e4
        p4_ref[...] = p4
        mask_ref[...] = mask

        @pl.when(pl.program_id(0) == 0)
        def _():
            cnt_ref[...] = jnp.zeros_like(cnt_ref)

        cnt_ref[...] = cnt_ref[...] + jnp.sum(mask, axis=0, keepdims=True)

    lanes_out = pl.BlockSpec((tl, LANES), lambda i: (i, 0))
    return pl.pallas_call(
        body,
        grid=(t // tl,),
        in_specs=[pl.BlockSpec((tl, d), lambda i: (i, 0)),
                  pl.BlockSpec((1, d), lambda i: (0, 0)),
                  pl.BlockSpec((1, 1, d), lambda i: (_mod_idx(layer, _seg(i, nct, lt), 3), 0, 0)),
                  pl.BlockSpec((1, 1, d), lambda i: (_mod_idx(layer, _seg(i, nct, lt), 4), 0, 0)),
                  pl.BlockSpec((d, LANES), lambda i: (0, 0)),
                  pl.BlockSpec((1, LANES), lambda i: (0, 0))],
        out_specs=[pl.BlockSpec((tl * pitch, LANES), lambda i: (i, 0)), lanes_out, lanes_out, lanes_out,
                   pl.BlockSpec((SUBLANES, LANES), lambda i: (0, 0))],
        out_shape=[jax.ShapeDtypeStruct((t * pitch, LANES), U32),
                   jax.ShapeDtypeStruct((t, LANES), I32),
                   jax.ShapeDtypeStruct((t, LANES), F32),
                   jax.ShapeDtypeStruct((t, LANES), F32),
                   jax.ShapeDtypeStruct((SUBLANES, LANES), F32)],
        compiler_params=_cp(("arbitrary",), 40),
        name="router",
    )(xs, g.reshape(1, d), mods, mods, rw, rb)


def _positions(mask, e4, cnt, n_experts, n_tiles, dims):
    t = mask.shape[0]
    tl = dims["TL"]
    te = EXPERT_TILE
    ntp = -(-n_tiles // SUBLANES) * SUBLANES

    def lane_cumsum(x, lane):
        sh = 1
        while sh < LANES:
            x = x + jnp.where(lane >= sh, pltpu.roll(x, sh, axis=1), 0.0)
            sh *= 2
        return x

    def body(mask_ref, e4_ref, cnt_ref, pos_ref, tab_ref, carry):
        i = pl.program_id(0)
        lane8 = lax.broadcasted_iota(I32, (SUBLANES, LANES), 1)
        cnt = cnt_ref[...]
        gp = jnp.ceil(cnt * (1.0 / te)) * te
        inc = lane_cumsum(gp, lane8)
        off = inc - gp

        @pl.when(i == 0)
        def _():
            carry[...] = jnp.zeros_like(carry)
            lane = lax.broadcasted_iota(I32, (ntp, LANES), 1)
            tile = lax.broadcasted_iota(I32, (ntp, LANES), 0).astype(F32)
            total = jnp.max(inc[0:1], axis=-1, keepdims=True)
            start = tile * te
            used = start < total
            start_c = jnp.minimum(start, total - te)
            ex = jnp.sum(jnp.where(inc[0:1] <= start_c, 1.0, 0.0), axis=-1, keepdims=True)
            ex = jnp.minimum(ex, n_experts - 1.0)
            hot = lane.astype(F32) == ex
            cnt_e = jnp.sum(jnp.where(hot, cnt[0:1], 0.0), axis=-1, keepdims=True)
            off_e = jnp.sum(jnp.where(hot, off[0:1], 0.0), axis=-1, keepdims=True)
            valid = jnp.where(used, jnp.clip(cnt_e - (start - off_e), 0.0, te), 0.0)
            tab = jnp.where(lane == 0, ex, jnp.where(lane == 1, valid, 0.0))
            tab_ref[...] = tab.astype(I32)

        m = mask_ref[...]
        r = lax.broadcasted_iota(I32, (tl, tl), 0)
        c = lax.broadcasted_iota(I32, (tl, tl), 1)
        tri = jnp.where(r > c, 1.0, 0.0).astype(BF16)
        rank = _dot(tri, m.astype(BF16))
        posf = off[0:1] + carry[0:1] + rank
        lane = lax.broadcasted_iota(I32, (tl, LANES), 1)
        e4 = e4_ref[...]
        pos4 = jnp.zeros((tl, LANES), F32)
        for k in range(TOP_K):
            ek = jnp.sum(jnp.where(lane == k, e4, 0), axis=-1, keepdims=True)
            pk = jnp.sum(jnp.where(lane == ek, posf, 0.0), axis=-1, keepdims=True)
            pos4 = jnp.where(lane == k, pk, pos4)
        pos_ref[...] = pos4.astype(I32)
        carry[...] = carry[...] + jnp.sum(m, axis=0, keepdims=True)

    rows = pl.BlockSpec((tl, LANES), lambda i: (i, 0))
    return pl.pallas_call(
        body,
        grid=(t // tl,),
        in_specs=[rows, rows, pl.BlockSpec((SUBLANES, LANES), lambda i: (0, 0))],
        out_specs=[rows, pl.BlockSpec((ntp, LANES), lambda i: (0, 0))],
        out_shape=[jax.ShapeDtypeStruct((t, LANES), I32), jax.ShapeDtypeStruct((ntp, LANES), I32)],
        scratch_shapes=[pltpu.VMEM((SUBLANES, LANES), F32)],
        compiler_params=_cp(("arbitrary",), 32),
        name="moe_positions",
    )(mask, e4, cnt)


def _invert(pos_flat, n_slots):
    n = pos_flat.shape[0]

    def body(pos_ref, inv_ref):
        def zero(s, c):
            inv_ref[s] = 0
            return c

        lax.fori_loop(0, n_slots, zero, 0, unroll=8)

        def put(q, c):
            inv_ref[pos_ref[q]] = lax.shift_right_logical(q, 2)
            return c

        lax.fori_loop(0, n, put, 0, unroll=8)

    assert TOP_K == 4
    return pl.pallas_call(
        body,
        in_specs=[pl.BlockSpec(memory_space=pltpu.SMEM)],
        out_specs=pl.BlockSpec(memory_space=pltpu.SMEM),
        out_shape=jax.ShapeDtypeStruct((n_slots,), I32),
        name="moe_invert",
    )(pos_flat)


def _prep_wgu(w):
    n_exp, d, n = w.shape
    tk = min(d, 1024)
    blk = min(2 * LANES, n)

    def body(w_ref, o_ref):
        r = lax.broadcasted_iota(I32, (blk, blk), 0)
        c = lax.broadcasted_iota(I32, (blk, blk), 1)
        src = jnp.where(c < blk // 2, 2 * c, 2 * (c - blk // 2) + 1)
        perm = jnp.where(r == src, 1.0, 0.0).astype(BF16)
        for b in range(n // blk):
            cols = slice(b * blk, (b + 1) * blk)
            o_ref[0, :, cols] = _dot(w_ref[0, :, cols].astype(BF16), perm).astype(BF16)

    return pl.pallas_call(
        body,
        grid=(n_exp, d // tk),
        in_specs=[pl.BlockSpec((1, tk, n), lambda e, k: (e, k, 0))],
        out_specs=pl.BlockSpec((1, tk, n), lambda e, k: (e, k, 0)),
        out_shape=jax.ShapeDtypeStruct((n_exp, d, n), BF16),
        compiler_params=_cp(("parallel", "parallel"), 32),
        name="prep_wgu",
    )(w)


def _deinterleave_bias(b):
    n_exp, n = b.shape
    blk = min(2 * LANES, n)
    return b.reshape(n_exp, n // blk, blk // 2, 2).transpose(0, 1, 3, 2).reshape(n_exp, n)


def _experts(hp, inv, tab_e, tab_v, wgu, bgu, wdn, bdn, n_tiles):
    n_exp, d, de2 = wgu.shape
    half = d // 2
    de = de2 // 2
    te = EXPERT_TILE
    rows_h, pitch_h = _token_rows(half)
    rows_y, pitch_y = _token_rows(d)
    blk = min(2 * LANES, de2)
    inv3 = inv.reshape(n_tiles, 1, te)

    def body(te_ref, tv_ref, cur_ref, nxt_ref, hp_ref, wgu_ref, bgu_ref, wdn_ref, bdn_ref,
             ys_ref, xbuf, sem):
        i = pl.program_id(0)
        slot = i % 2

        def row_copy(tok, sl, r):
            return pltpu.make_async_copy(hp_ref.at[pl.ds(tok * pitch_h, rows_h)],
                                         xbuf.at[sl, pl.ds(r * pitch_h, rows_h)], sem.at[sl])

        def issue(idx_ref, sl):
            def f(r, c):
                row_copy(idx_ref[0, 0, r], sl, r).start()
                return c

            lax.fori_loop(0, te, f, 0, unroll=8)

        @pl.when(jnp.logical_and(i == 0, tv_ref[0] > 0))
        def _():
            issue(cur_ref, 0)

        nxt = jnp.minimum(i + 1, n_tiles - 1)

        @pl.when(jnp.logical_and(i + 1 < n_tiles, tv_ref[nxt] > 0))
        def _():
            issue(nxt_ref, 1 - slot)

        @pl.when(tv_ref[i] > 0)
        def _():
            def w(r, c):
                row_copy(0, slot, r).wait()
                return c

            lax.fori_loop(0, te, w, 0, unroll=8)
            los, his = [], []
            for cidx in range(rows_h):
                words = xbuf[slot, pl.ds(cidx, te, stride=pitch_h), :]
                los.append(pltpu.bitcast(lax.shift_left(words, jnp.uint32(16)), F32).astype(BF16))
                his.append(pltpu.bitcast(words & jnp.uint32(0xFFFF0000), F32).astype(BF16))
            lo = jnp.concatenate(los, axis=1)
            hi = jnp.concatenate(his, axis=1)
            gu = _dot(lo, wgu_ref[0, :half, :]) + _dot(hi, wgu_ref[0, half:, :]) + bgu_ref[0]
            acts = []
            for b in range(de2 // blk):
                gate = jnp.minimum(gu[:, b * blk:b * blk + blk // 2], SWIGLU_LIMIT)
                up = jnp.clip(gu[:, b * blk + blk // 2:(b + 1) * blk], -SWIGLU_LIMIT, SWIGLU_LIMIT)
                acts.append(((up + 1.0) * gate * jax.nn.sigmoid(SWIGLU_ALPHA * gate)).astype(BF16))
            out = _dot(jnp.concatenate(acts, axis=1), wdn_ref[0]) + bdn_ref[0]
            for cidx in range(pitch_y):
                chunk = out[:, cidx * LANES:(cidx + 1) * LANES] if cidx < rows_y else jnp.zeros((te, LANES), F32)
                ys_ref[pl.ds(cidx, te, stride=pitch_y), :] = chunk

        @pl.when(tv_ref[i] == 0)
        def _():
            ys_ref[...] = jnp.zeros_like(ys_ref)

    grid_spec = pltpu.PrefetchScalarGridSpec(
        num_scalar_prefetch=2,
        grid=(n_tiles,),
        in_specs=[pl.BlockSpec((1, 1, te), lambda i, e, v: (i, 0, 0), memory_space=pltpu.SMEM),
                  pl.BlockSpec((1, 1, te), lambda i, e, v: (jnp.minimum(i + 1, n_tiles - 1), 0, 0),
                               memory_space=pltpu.SMEM),
                  pl.BlockSpec(memory_space=pl.ANY),
                  pl.BlockSpec((1, d, de2), lambda i, e, v: (e[i], 0, 0)),
                  pl.BlockSpec((1, 1, de2), lambda i, e, v: (e[i], 0, 0)),
                  pl.BlockSpec((1, de, d), lambda i, e, v: (e[i], 0, 0)),
                  pl.BlockSpec((1, 1, d), lambda i, e, v: (e[i], 0, 0))],
        out_specs=pl.BlockSpec((te * pitch_y, LANES), lambda i, e, v: (i, 0)),
        scratch_shapes=[pltpu.VMEM((2, te * pitch_h, LANES), U32), pltpu.SemaphoreType.DMA((2,))],
    )
    return pl.pallas_call(
        body,
        grid_spec=grid_spec,
        out_shape=jax.ShapeDtypeStruct((n_tiles * te * pitch_y, LANES), F32),
        compiler_params=_cp(("arbitrary",), 56),
        name="moe_experts",
    )(tab_e, tab_v, inv3, inv3, hp, wgu, bgu.reshape(n_exp, 1, de2), wdn, bdn.reshape(n_exp, 1, d))


def _combine(xs, ys, pos_flat, p4, mods, layer, dims, final_g):
    t, d = xs.shape
    tc = COMBINE_TILE
    nq = tc * TOP_K
    nct_rows = dims["B"] * dims["CTX"]
    nct, lt = nct_rows // tc, dims["L"] // tc
    final = final_g is not None
    skip = nct if final else 0
    n_steps = t // tc - skip
    pos3 = pos_flat.reshape(t // tc, 1, nq)
    rows_y, pitch_y = _token_rows(d)

    def body(cur_ref, nxt_ref, x_ref, p_ref, g_ref, fg_ref, ys_ref, o_ref, gbuf, sem):
        i = pl.program_id(0)
        slot = i % 2

        def row_copy(src, sl, q):
            dst = ((q % TOP_K) * tc + q // TOP_K) * pitch_y
            return pltpu.make_async_copy(ys_ref.at[pl.ds(src * pitch_y, rows_y)],
                                         gbuf.at[sl, pl.ds(dst, rows_y)], sem.at[sl])

        def issue(idx_ref, sl):
            def f(q, c):
                row_copy(idx_ref[0, 0, q], sl, q).start()
                return c

            lax.fori_loop(0, nq, f, 0, unroll=8)

        @pl.when(i == 0)
        def _():
            issue(cur_ref, 0)

        @pl.when(i + 1 < n_steps)
        def _():
            issue(nxt_ref, 1 - slot)

        def w(q, c):
            row_copy(0, slot, q).wait()
            return c

        lax.fori_loop(0, nq, w, 0, unroll=8)
        p = p_ref[...]
        pk = [p[:, k:k + 1] for k in range(TOP_K)]
        ys_cols = []
        for cidx in range(rows_y):
            y = pk[0] * gbuf[slot, pl.ds(cidx, tc, stride=pitch_y), :]
            for k in range(1, TOP_K):
                y = y + pk[k] * gbuf[slot, pl.ds(k * tc * pitch_y + cidx, tc, stride=pitch_y), :]
            ys_cols.append(y)
        xn = x_ref[...] + g_ref[0] * jnp.concatenate(ys_cols, axis=1)
        if final:
            xn = xn * lax.rsqrt(jnp.mean(xn * xn, axis=-1, keepdims=True) + EPS) * fg_ref[...]
        o_ref[...] = xn

    fg = jnp.ones((1, d), F32) if final_g is None else final_g.reshape(1, d)
    return pl.pallas_call(
        body,
        grid=(n_steps,),
        in_specs=[pl.BlockSpec((1, 1, nq), lambda i: (i + skip, 0, 0), memory_space=pltpu.SMEM),
                  pl.BlockSpec((1, 1, nq), lambda i: (jnp.minimum(i + 1, n_steps - 1) + skip, 0, 0),
                               memory_space=pltpu.SMEM),
                  pl.BlockSpec((tc, d), lambda i: (i + skip, 0)),
                  pl.BlockSpec((tc, LANES), lambda i: (i + skip, 0)),
                  pl.BlockSpec((1, 1, d), lambda i: (_mod_idx(layer, _seg(i + skip, nct, lt), 5), 0, 0)),
                  pl.BlockSpec((1, d), lambda i: (0, 0)),
                  pl.BlockSpec(memory_space=pl.ANY)],
        out_specs=pl.BlockSpec((tc, d), lambda i: (i, 0)),
        out_shape=jax.ShapeDtypeStruct((n_steps * tc, d), F32),
        scratch_shapes=[pltpu.VMEM((2, TOP_K * tc * pitch_y, LANES), F32), pltpu.SemaphoreType.DMA((2,))],
        compiler_params=_cp(("arbitrary",), 40),
        name="moe_combine",
    )(pos3, pos3, xs, p4, mods, fg, ys)


def kernel(x, c, ctx, c_ctx, norm1_g, norm2_g, w_ada, b_ada, w_in, conv_w, conv_b, gn_g, gn_b, lru_conv_w, lru_conv_b, lru_w_a, lru_b_a, lru_w_x, lru_b_x, lru_lambda, mix_norm_g, w_out, router_w, router_b, exp_w_gu, exp_b_gu, exp_w_down, exp_b_down, final_norm_g):
    nb, seq, d = x.shape
    ctx_len = ctx.shape[1]
    depth = w_in.shape[0]
    cw = conv_w.shape[2]
    n_exp = router_w.shape[2]
    de = exp_w_down.shape[2]
    tl = ctx_len
    t = nb * (ctx_len + seq)
    assert nb + 1 <= MOD_ROWS and seq % tl == 0 and n_exp <= LANES
    tm = min(512, nb * ctx_len)
    assert (nb * ctx_len) % tm == 0 and seq % tm == 0
    assert (nb * ctx_len) % COMBINE_TILE == 0 and seq % COMBINE_TILE == 0
    dims = {"B": nb, "CTX": ctx_len, "L": seq, "TL": tl, "NCT": nb, "LT": seq // tl, "TM": tm}
    n_tiles = (t * TOP_K) // EXPERT_TILE + n_exp
    n_slots = n_tiles * EXPERT_TILE

    cvec = jnp.concatenate([c_ctx[None], c, jnp.zeros((MOD_ROWS - 1 - nb, d), F32)], axis=0)
    mods = _adaln(cvec, w_ada, b_ada).reshape(depth * MOD_ROWS * N_MOD, 1, d)
    xs = jnp.concatenate([ctx.reshape(nb * ctx_len, d), x.reshape(nb * seq, d)], axis=0)

    for l in range(depth):
        last = l == depth - 1
        hl = _prenorm(xs, norm1_g[l], mods, l, dims)
        zl = _matmul(hl, w_in[l], tm, min(512, w_in.shape[2]), "in_proj")
        mix_c = _conv_module(zl, conv_w[l], conv_b[l], gn_g[l], gn_b[l], mix_norm_g[l, :cw], dims)
        af, bf, ab, bb = _lru_gates(zl, lru_conv_w[l], lru_conv_b[l], lru_w_a[l], lru_b_a[l],
                                    lru_w_x[l], lru_b_x[l], lru_lambda[l], dims)
        hf, hb = _scan(af, bf, ab, bb, dims)
        mix_r = _lru_out(zl, hf, hb, mix_norm_g[l, cw:], dims)
        xs = _wout_residual(mix_c, mix_r, w_out[l], xs, mods, l, dims)

        rw = jnp.pad(router_w[l], ((0, 0), (0, LANES - n_exp)))
        rb = jnp.pad(router_b[l], (0, LANES - n_exp), constant_values=NEG_BIG).reshape(1, LANES)
        hp, e4, p4, mask, cnt = _router(xs, norm2_g[l], mods, l, rw, rb, dims)
        pos4, tab = _positions(mask, e4, cnt, n_exp, n_tiles, dims)
        pos_flat = pos4[:, :TOP_K].reshape(t * TOP_K)
        inv = _invert(pos_flat, n_slots)
        ys = _experts(hp, inv, tab[:n_tiles, 0], tab[:n_tiles, 1],
                      _prep_wgu(exp_w_gu[l]), _deinterleave_bias(exp_b_gu[l]),
                      exp_w_down[l].astype(BF16), exp_b_down[l], n_tiles)
        xs = _combine(xs, ys, pos_flat, p4, mods, l, dims, final_norm_g if last else None)

    return xs.reshape(nb, seq, d)
```

```python
import jax
import jax.numpy as jnp
from jax import lax
from jax.experimental import pallas as pl
from jax.experimental.pallas import tpu as pltpu

GRID_W = 64
CONV_GROUPS = 16
TOP_K = 4
N_MOD = 6
EPS = 1e-6
LRU_C = 8.0
SWIGLU_LIMIT = 7.0
SWIGLU_ALPHA = 1.702

LANES = 128
SUBLANES = 8
MOD_ROWS = 8
HALO = 16
CONV_ROWS = 64
EXPERT_TILE = 256
COMBINE_TILE = 64
NEG_BIG = -1e30

F32 = jnp.float32
BF16 = jnp.bfloat16
U32 = jnp.uint32
I32 = jnp.int32


def _cp(sems, vmem_mb):
    return pltpu.CompilerParams(dimension_semantics=sems, vmem_limit_bytes=vmem_mb << 20)


def _seg(i, n_ctx_tiles, lat_tiles):
    return jnp.where(i < n_ctx_tiles, 0, 1 + (i - n_ctx_tiles) // lat_tiles)


def _mod_idx(layer, seg, j):
    return (layer * MOD_ROWS + seg) * N_MOD + j


def _dot(a, b):
    return jnp.dot(a, b, preferred_element_type=F32)


def _sigmoid(x):
    return 0.5 * jnp.tanh(0.5 * x) + 0.5


def _token_rows(width):
    rows = width // LANES
    return rows, rows + 4


def _adaln(cvec, w_ada, b_ada):
    depth, d, n = w_ada.shape
    tn = min(n, 1024)

    def body(c_ref, w_ref, b_ref, o_ref):
        c = c_ref[...]
        s = (c * jax.nn.sigmoid(c)).astype(BF16)
        o_ref[0] = _dot(s, w_ref[0].astype(BF16)) + b_ref[0]

    return pl.pallas_call(
        body,
        grid=(depth, n // tn),
        in_specs=[pl.BlockSpec((MOD_ROWS, d), lambda l, j: (0, 0)),
                  pl.BlockSpec((1, d, tn), lambda l, j: (l, 0, j)),
                  pl.BlockSpec((1, 1, tn), lambda l, j: (l, 0, j))],
        out_specs=pl.BlockSpec((1, MOD_ROWS, tn), lambda l, j: (l, 0, j)),
        out_shape=jax.ShapeDtypeStruct((depth, MOD_ROWS, n), F32),
        compiler_params=_cp(("parallel", "parallel"), 48),
        name="adaln",
    )(cvec, w_ada, b_ada.reshape(depth, 1, n))


def _prenorm(xs, g, mods, layer, dims):
    t, d = xs.shape
    tl, nct, lt = dims["TL"], dims["NCT"], dims["LT"]

    def body(x_ref, g_ref, sh_ref, sc_ref, o_ref):
        x = x_ref[...]
        ms = jnp.mean(x * x, axis=-1, keepdims=True)
        y = x * lax.rsqrt(ms + EPS) * g_ref[...]
        o_ref[...] = (y * (1.0 + sc_ref[0]) + sh_ref[0]).astype(o_ref.dtype)

    return pl.pallas_call(
        body,
        grid=(t // tl,),
        in_specs=[pl.BlockSpec((tl, d), lambda i: (i, 0)),
                  pl.BlockSpec((1, d), lambda i: (0, 0)),
                  pl.BlockSpec((1, 1, d), lambda i: (_mod_idx(layer, _seg(i, nct, lt), 0), 0, 0)),
                  pl.BlockSpec((1, 1, d), lambda i: (_mod_idx(layer, _seg(i, nct, lt), 1), 0, 0))],
        out_specs=pl.BlockSpec((tl, d), lambda i: (i, 0)),
        out_shape=jax.ShapeDtypeStruct((t, d), BF16),
        compiler_params=_cp(("parallel",), 32),
        name="prenorm1",
    )(xs, g.reshape(1, d), mods, mods)


def _matmul(a, w, layer, tm, tn, name):
    m, k = a.shape
    n = w.shape[2]

    def body(a_ref, w_ref, o_ref):
        o_ref[...] = _dot(a_ref[...], w_ref[...])

    return pl.pallas_call(
        body,
        grid=(n // tn, m // tm),
        in_specs=[pl.BlockSpec((tm, k), lambda j, i: (i, 0)),
                  pl.BlockSpec((None, k, tn), lambda j, i: (layer, 0, j))],
        out_specs=pl.BlockSpec((tm, tn), lambda j, i: (i, j)),
        out_shape=jax.ShapeDtypeStruct((m, n), F32),
        compiler_params=_cp(("parallel", "parallel"), 48),
        name=name,
    )(a, w)


def _conv_module(zl, conv_w, conv_b, gn_g, gn_b, mix_g, dims):
    t = zl.shape[0]
    k_taps, cw = conv_w.shape
    pad = (k_taps - 1) // 2
    tl, nct, ctx_len = dims["TL"], dims["NCT"], dims["CTX"]
    assert cw // CONV_GROUPS == LANES and pad <= HALO
    assert tl % GRID_W == 0 and GRID_W % CONV_ROWS == 0 and ctx_len % CONV_ROWS == 0
    upad_rows = max((tl // GRID_W) * (GRID_W + 2 * HALO), ctx_len + 2 * HALO)

    def body(za_ref, zg_ref, w_ref, b_ref, gg_ref, gb_ref, mg_ref, o_ref, upad, ybuf, ssbuf):
        i = pl.program_id(0)
        upad[...] = jnp.zeros_like(upad)
        ssbuf[...] = jnp.zeros_like(ssbuf)

        def path(seg_len):
            nseg = tl // seg_len
            stride = seg_len + 2 * HALO

            def chunk(c, carry):
                ln = pl.ds(pl.multiple_of(c * LANES, LANES), LANES)
                u = za_ref[:, ln] * jax.nn.sigmoid(zg_ref[:, ln])
                for s in range(nseg):
                    upad[s * stride + HALO:s * stride + HALO + seg_len, :] = u[s * seg_len:(s + 1) * seg_len]
                bias = b_ref[:, ln]
                gg = gg_ref[:, ln]
                gb = gb_ref[:, ln]
                for s in range(nseg):
                    for r0 in range(0, seg_len, CONV_ROWS):
                        base = s * stride + HALO + r0 - pad
                        acc = jnp.broadcast_to(bias, (CONV_ROWS, LANES))
                        for k in range(k_taps):
                            acc = acc + w_ref[pl.ds(k, 1), ln] * upad[base + k:base + k + CONV_ROWS, :]
                        mu = jnp.mean(acc, axis=-1, keepdims=True)
                        dv = acc - mu
                        var = jnp.mean(dv * dv, axis=-1, keepdims=True)
                        y = dv * lax.rsqrt(var + EPS) * gg + gb
                        y = y * jax.nn.sigmoid(y)
                        rows = pl.ds(s * seg_len + r0, CONV_ROWS)
                        ybuf[rows, ln] = y
                        ssbuf[rows, :] = ssbuf[rows, :] + jnp.sum(y * y, axis=-1, keepdims=True)
                return carry

            lax.fori_loop(0, cw // LANES, chunk, 0)

        @pl.when(i < nct)
        def _():
            path(ctx_len)

        @pl.when(i >= nct)
        def _():
            path(GRID_W)

        inv = lax.rsqrt(ssbuf[:, 0:1] * (1.0 / cw) + EPS)
        o_ref[...] = (ybuf[...] * inv * mg_ref[...]).astype(o_ref.dtype)

    vec = pl.BlockSpec((1, cw), lambda i: (0, 0))
    return pl.pallas_call(
        body,
        grid=(t // tl,),
        in_specs=[pl.BlockSpec((tl, cw), lambda i: (i, 0)),
                  pl.BlockSpec((tl, cw), lambda i: (i, 1)),
                  pl.BlockSpec((k_taps, cw), lambda i: (0, 0)),
                  vec, vec, vec, vec],
        out_specs=pl.BlockSpec((tl, cw), lambda i: (i, 0)),
        out_shape=jax.ShapeDtypeStruct((t, cw), BF16),
        scratch_shapes=[pltpu.VMEM((upad_rows, LANES), F32),
                        pltpu.VMEM((tl, cw), F32),
                        pltpu.VMEM((tl, LANES), F32)],
        compiler_params=_cp(("parallel",), 40),
        name="conv_module",
    )(zl, zl, conv_w, conv_b.reshape(1, cw), gn_g.reshape(1, cw), gn_b.reshape(1, cw), mix_g.reshape(1, cw))


def _lru_gates(zl, cw, cb, wa, ba, wx, bx, lam, dims):
    t = zl.shape[0]
    ndir, kc, lw = cw.shape
    heads, hd = wa.shape[1], wa.shape[2]
    tl, nct, lt = dims["TL"], dims["NCT"], dims["LT"]
    assert hd == LANES and heads * hd == lw and ndir == 2 and kc - 1 <= SUBLANES
    nrb = t // SUBLANES
    tls = tl // SUBLANES

    def body(x_ref, p_ref, n_ref, cw_ref, cb_ref, wa_ref, ba_ref, wx_ref, bx_ref, lam_ref,
             af_ref, bf_ref, ab_ref, bb_ref, xp):
        i = pl.program_id(0)
        is_ctx = i < nct
        j = jnp.where(is_ctx, 0, (i - nct) % lt)
        first = jnp.logical_or(is_ctx, j == 0)
        last = jnp.logical_or(is_ctx, j == lt - 1)
        outs = ((af_ref, bf_ref), (ab_ref, bb_ref))

        def head(h, carry):
            ln = pl.ds(pl.multiple_of(h * LANES, LANES), LANES)
            xp[0:SUBLANES, :] = jnp.where(first, 0.0, p_ref[:, ln])
            xp[SUBLANES:SUBLANES + tl, :] = x_ref[:, ln]
            xp[SUBLANES + tl:2 * SUBLANES + tl, :] = jnp.where(last, 0.0, n_ref[:, ln])
            win = {o: xp[SUBLANES + o:SUBLANES + o + tl, :] for o in range(-(kc - 1), kc)}
            for d in range(ndir):
                xc = jnp.broadcast_to(cb_ref[pl.ds(d, 1), ln], (tl, LANES))
                for k in range(kc):
                    off = k - (kc - 1) if d == 0 else (kc - 1) - k
                    xc = xc + cw_ref[d, pl.ds(k, 1), ln] * win[off]
                xcb = xc.astype(BF16)
                r = _sigmoid(_dot(xcb, wa_ref[d, h].astype(BF16)) + ba_ref[pl.ds(d, 1), ln])
                ig = _sigmoid(_dot(xcb, wx_ref[d, h].astype(BF16)) + bx_ref[pl.ds(d, 1), ln])
                sp = jax.nn.softplus(-lam_ref[pl.ds(d, 1), ln])
                log_a = (-LRU_C * sp) * r
                a_ref, b_ref = outs[d]
                a = jnp.exp(log_a)
                a_ref[:, ln] = a
                b_ref[:, ln] = jnp.sqrt(1.0 - a * a) * (ig * xc)
            return carry

        lax.fori_loop(0, heads, head, 0)

    xcol = 2
    full2 = pl.BlockSpec((ndir, lw), lambda i: (0, 0))
    full4 = pl.BlockSpec((ndir, heads, hd, hd), lambda i: (0, 0, 0, 0))
    out = pl.BlockSpec((tl, lw), lambda i: (i, 0))
    shp = jax.ShapeDtypeStruct((t, lw), F32)
    return pl.pallas_call(
        body,
        grid=(t // tl,),
        in_specs=[pl.BlockSpec((tl, lw), lambda i: (i, xcol)),
                  pl.BlockSpec((SUBLANES, lw), lambda i: (jnp.maximum(i * tls - 1, 0), xcol)),
                  pl.BlockSpec((SUBLANES, lw), lambda i: (jnp.minimum((i + 1) * tls, nrb - 1), xcol)),
                  pl.BlockSpec((ndir, kc, lw), lambda i: (0, 0, 0)),
                  full2, full4, full2, full4, full2, full2],
        out_specs=[out, out, out, out],
        out_shape=[shp, shp, shp, shp],
        scratch_shapes=[pltpu.VMEM((tl + 2 * SUBLANES, LANES), F32)],
        compiler_params=_cp(("parallel",), 40),
        name="lru_gates",
    )(zl, zl, zl, cw, cb, wa, ba, wx, bx, lam)


def _scan(af, bf, ab, bb, dims):
    t, lw = af.shape
    hs = lw // LANES
    tl, nct, lt, nb = dims["TL"], dims["NCT"], dims["LT"], dims["B"]
    assert nct == nb

    def v(z):
        return z.reshape(t, hs, LANES)

    def fidx(b, s):
        return jnp.where(s == 0, b, nct + b * lt + s - 1)

    def bidx(b, s):
        return jnp.where(s == 0, b, nct + b * lt + lt - s)

    def body(af_ref, bf_ref, ab_ref, bb_ref, hf_ref, hb_ref, hc):
        @pl.when(pl.program_id(1) == 0)
        def _():
            hc[...] = jnp.zeros_like(hc)

        def block(q, carry):
            hf, hb = carry
            tf = pl.multiple_of(q * SUBLANES, SUBLANES)
            tb = pl.multiple_of(tl - SUBLANES - q * SUBLANES, SUBLANES)
            a_f, b_f = af_ref[pl.ds(tf, SUBLANES)], bf_ref[pl.ds(tf, SUBLANES)]
            a_b, b_b = ab_ref[pl.ds(tb, SUBLANES)], bb_ref[pl.ds(tb, SUBLANES)]
            of, ob = [], []
            for k in range(SUBLANES):
                hf = a_f[k] * hf + b_f[k]
                of.append(hf)
                kb = SUBLANES - 1 - k
                hb = a_b[kb] * hb + b_b[kb]
                ob.append(hb)
            hf_ref[pl.ds(tf, SUBLANES)] = jnp.stack(of)
            hb_ref[pl.ds(tb, SUBLANES)] = jnp.stack(ob[::-1])
            return hf, hb

        hf, hb = lax.fori_loop(0, tl // SUBLANES, block, (hc[0], hc[1]))
        hc[0] = hf
        hc[1] = hb

    fspec = pl.BlockSpec((tl, hs, LANES), lambda b, s: (fidx(b, s), 0, 0))
    bspec = pl.BlockSpec((tl, hs, LANES), lambda b, s: (bidx(b, s), 0, 0))
    shp = jax.ShapeDtypeStruct((t, hs, LANES), F32)
    hf, hb = pl.pallas_call(
        body,
        grid=(nb, 1 + lt),
        in_specs=[fspec, fspec, bspec, bspec],
        out_specs=[fspec, bspec],
        out_shape=[shp, shp],
        scratch_shapes=[pltpu.VMEM((2, hs, LANES), F32)],
        compiler_params=_cp(("parallel", "arbitrary"), 40),
        name="lru_scan",
    )(v(af), v(bf), v(ab), v(bb))
    return hf.reshape(t, lw), hb.reshape(t, lw)


def _lru_out(zl, hf, hb, mix_g, dims):
    t, lw = hf.shape
    tl = dims["TL"]
    gcol = 3

    def body(g_ref, hf_ref, hb_ref, mg_ref, o_ref):
        y = jax.nn.gelu(g_ref[...], approximate=True) * (hf_ref[...] + hb_ref[...])
        inv = lax.rsqrt(jnp.mean(y * y, axis=-1, keepdims=True) + EPS)
        o_ref[...] = (y * inv * mg_ref[...]).astype(o_ref.dtype)

    row = pl.BlockSpec((tl, lw), lambda i: (i, 0))
    return pl.pallas_call(
        body,
        grid=(t // tl,),
        in_specs=[pl.BlockSpec((tl, lw), lambda i: (i, gcol)), row, row,
                  pl.BlockSpec((1, lw), lambda i: (0, 0))],
        out_specs=row,
        out_shape=jax.ShapeDtypeStruct((t, lw), BF16),
        compiler_params=_cp(("parallel",), 40),
        name="lru_out",
    )(zl, hf, hb, mix_g.reshape(1, lw))


def _wout_residual(mix_c, mix_r, w_out, xs, mods, layer, dims):
    t, d = xs.shape
    kc = mix_c.shape[1]
    kr = mix_r.shape[1]
    assert kc == kr
    tm, tn = dims["TM"], min(d, 1024)
    nct, lt = dims["B"] * dims["CTX"] // tm, dims["L"] // tm

    def body(a1_ref, a2_ref, w1_ref, w2_ref, x_ref, g_ref, o_ref):
        acc = _dot(a1_ref[...], w1_ref[...]) + _dot(a2_ref[...], w2_ref[...])
        o_ref[...] = x_ref[...] + g_ref[0] * acc

    return pl.pallas_call(
        body,
        grid=(d // tn, t // tm),
        in_specs=[pl.BlockSpec((tm, kc), lambda j, i: (i, 0)),
                  pl.BlockSpec((tm, kr), lambda j, i: (i, 0)),
                  pl.BlockSpec((None, kc, tn), lambda j, i: (layer, 0, j)),
                  pl.BlockSpec((None, kr, tn), lambda j, i: (layer, 1, j)),
                  pl.BlockSpec((tm, tn), lambda j, i: (i, j)),
                  pl.BlockSpec((1, 1, tn), lambda j, i: (_mod_idx(layer, _seg(i, nct, lt), 2), 0, j))],
        out_specs=pl.BlockSpec((tm, tn), lambda j, i: (i, j)),
        out_shape=jax.ShapeDtypeStruct((t, d), F32),
        compiler_params=_cp(("parallel", "parallel"), 48),
        name="wout_residual",
    )(mix_c, mix_r, w_out, w_out, xs, mods)


def _router(xs, g, mods, layer, rw, rb, dims):
    t, d = xs.shape
    half = d // 2
    rows_h, pitch = _token_rows(half)
    tl, nct, lt = dims["TL"], dims["NCT"], dims["LT"]

    def body(x_ref, g_ref, sh_ref, sc_ref, rw_ref, rb_ref, hp_ref, e4_ref, p4_ref, mask_ref, cnt_ref):
        x = x_ref[...]
        ms = jnp.mean(x * x, axis=-1, keepdims=True)
        h = (x * lax.rsqrt(ms + EPS) * g_ref[...]) * (1.0 + sc_ref[0]) + sh_ref[0]
        hh = h.astype(BF16)
        hh32 = hh.astype(F32)
        bits = pltpu.bitcast(hh32, U32)
        words = (bits[:, half:] & jnp.uint32(0xFFFF0000)) | lax.shift_right_logical(bits[:, :half], jnp.uint32(16))
        for cidx in range(pitch):
            chunk = words[:, cidx * LANES:(cidx + 1) * LANES] if cidx < rows_h else jnp.zeros((tl, LANES), U32)
            hp_ref[pl.ds(cidx, tl, stride=pitch), :] = chunk

        w = rw_ref[...]
        wh = w.astype(BF16)
        wl = (w - wh.astype(F32)).astype(BF16)
        hl = (h - hh32).astype(BF16)
        logits = _dot(hh, wh) + (_dot(hh, wl) + _dot(hl, wh)) + rb_ref[...]

        lane = lax.broadcasted_iota(I32, (tl, LANES), 1)
        vals = logits
        tops, hots = [], []
        e4 = jnp.zeros((tl, LANES), I32)
        for k in range(TOP_K):
            m = jnp.max(vals, axis=-1, keepdims=True)
            sel = jnp.min(jnp.where(vals == m, lane, LANES), axis=-1, keepdims=True)
            hot = lane == sel
            tops.append(m)
            hots.append(hot)
            vals = jnp.where(hot, -jnp.inf, vals)
            e4 = jnp.where(lane == k, sel, e4)
        ex = [jnp.exp(m - tops[0]) for m in tops]
        den = ex[0]
        for e in ex[1:]:
            den = den + e
        p4 = jnp.zeros((tl, LANES), F32)
        mask = jnp.zeros((tl, LANES), F32)
        for k in range(TOP_K):
            p4 = jnp.where(lane == k, ex[k] / den, p4)
            mask = jnp.where(hots[k], 1.0, mask)
        e4_ref[...] = e4
        p4_ref[...] = p4
        mask_ref[...] = mask

        @pl.when(pl.program_id(0) == 0)
        def _():
            cnt_ref[...] = jnp.zeros_like(cnt_ref)

        cnt_ref[...] = cnt_ref[...] + jnp.sum(mask, axis=0, keepdims=True)

    lanes_out = pl.BlockSpec((tl, LANES), lambda i: (i, 0))
    return pl.pallas_call(
        body,
        grid=(t // tl,),
        in_specs=[pl.BlockSpec((tl, d), lambda i: (i, 0)),
                  pl.BlockSpec((1, d), lambda i: (0, 0)),
                  pl.BlockSpec((1, 1, d), lambda i: (_mod_idx(layer, _seg(i, nct, lt), 3), 0, 0)),
                  pl.BlockSpec((1, 1, d), lambda i: (_mod_idx(layer, _seg(i, nct, lt), 4), 0, 0)),
                  pl.BlockSpec((d, LANES), lambda i: (0, 0)),
                  pl.BlockSpec((1, LANES), lambda i: (0, 0))],
        out_specs=[pl.BlockSpec((tl * pitch, LANES), lambda i: (i, 0)), lanes_out, lanes_out, lanes_out,
                   pl.BlockSpec((SUBLANES, LANES), lambda i: (0, 0))],
        out_shape=[jax.ShapeDtypeStruct((t * pitch, LANES), U32),
                   jax.ShapeDtypeStruct((t, LANES), I32),
                   jax.ShapeDtypeStruct((t, LANES), F32),
                   jax.ShapeDtypeStruct((t, LANES), F32),
                   jax.ShapeDtypeStruct((SUBLANES, LANES), F32)],
        compiler_params=_cp(("arbitrary",), 40),
        name="router",
    )(xs, g.reshape(1, d), mods, mods, rw, rb)


def _positions(mask, e4, cnt, n_experts, n_tiles, dims):
    t = mask.shape[0]
    tl = dims["TL"]
    te = EXPERT_TILE
    ntp = -(-n_tiles // SUBLANES) * SUBLANES

    def lane_cumsum(x, lane):
        sh = 1
        while sh < LANES:
            x = x + jnp.where(lane >= sh, pltpu.roll(x, sh, axis=1), 0.0)
            sh *= 2
        return x

    def body(mask_ref, e4_ref, cnt_ref, pos_ref, tab_ref, carry):
        i = pl.program_id(0)
        lane8 = lax.broadcasted_iota(I32, (SUBLANES, LANES), 1)
        cnt = cnt_ref[...]
        gp = jnp.ceil(cnt * (1.0 / te)) * te
        inc = lane_cumsum(gp, lane8)
        off = inc - gp

        @pl.when(i == 0)
        def _():
            carry[...] = jnp.zeros_like(carry)
            lane = lax.broadcasted_iota(I32, (ntp, LANES), 1)
            tile = lax.broadcasted_iota(I32, (ntp, LANES), 0).astype(F32)
            total = jnp.max(inc[0:1], axis=-1, keepdims=True)
            start = tile * te
            used = start < total
            start_c = jnp.minimum(start, total - te)
            ex = jnp.sum(jnp.where(inc[0:1] <= start_c, 1.0, 0.0), axis=-1, keepdims=True)
            ex = jnp.minimum(ex, n_experts - 1.0)
            hot = lane.astype(F32) == ex
            cnt_e = jnp.sum(jnp.where(hot, cnt[0:1], 0.0), axis=-1, keepdims=True)
            off_e = jnp.sum(jnp.where(hot, off[0:1], 0.0), axis=-1, keepdims=True)
            valid = jnp.where(used, jnp.clip(cnt_e - (start - off_e), 0.0, te), 0.0)
            tab = jnp.where(lane == 0, ex, jnp.where(lane == 1, valid, 0.0))
            tab_ref[...] = tab.astype(I32)

        m = mask_ref[...]
        r = lax.broadcasted_iota(I32, (tl, tl), 0)
        c = lax.broadcasted_iota(I32, (tl, tl), 1)
        tri = jnp.where(r > c, 1.0, 0.0).astype(BF16)
        rank = _dot(tri, m.astype(BF16))
        posf = off[0:1] + carry[0:1] + rank
        lane = lax.broadcasted_iota(I32, (tl, LANES), 1)
        e4 = e4_ref[...]
        pos4 = jnp.zeros((tl, LANES), F32)
        for k in range(TOP_K):
            ek = jnp.sum(jnp.where(lane == k, e4, 0), axis=-1, keepdims=True)
            pk = jnp.sum(jnp.where(lane == ek, posf, 0.0), axis=-1, keepdims=True)
            pos4 = jnp.where(lane == k, pk, pos4)
        pos_ref[...] = pos4.astype(I32)
        carry[...] = carry[...] + jnp.sum(m, axis=0, keepdims=True)

    rows = pl.BlockSpec((tl, LANES), lambda i: (i, 0))
    return pl.pallas_call(
        body,
        grid=(t // tl,),
        in_specs=[rows, rows, pl.BlockSpec((SUBLANES, LANES), lambda i: (0, 0))],
        out_specs=[rows, pl.BlockSpec((ntp, LANES), lambda i: (0, 0))],
        out_shape=[jax.ShapeDtypeStruct((t, LANES), I32), jax.ShapeDtypeStruct((ntp, LANES), I32)],
        scratch_shapes=[pltpu.VMEM((SUBLANES, LANES), F32)],
        compiler_params=_cp(("arbitrary",), 32),
        name="moe_positions",
    )(mask, e4, cnt)


def _invert(pos_flat, n_slots):
    n = pos_flat.shape[0]

    def body(pos_ref, init_ref, inv_ref):
        del init_ref

        def put(q, c):
            inv_ref[pos_ref[q]] = lax.shift_right_logical(q, 2)
            return c

        lax.fori_loop(0, n, put, 0, unroll=8)

    assert TOP_K == 4
    return pl.pallas_call(
        body,
        in_specs=[pl.BlockSpec(memory_space=pltpu.SMEM), pl.BlockSpec(memory_space=pltpu.SMEM)],
        out_specs=pl.BlockSpec(memory_space=pltpu.SMEM),
        out_shape=jax.ShapeDtypeStruct((n_slots,), I32),
        input_output_aliases={1: 0},
        name="moe_invert",
    )(pos_flat, jnp.zeros((n_slots,), I32))


def _prep_wgu(w, layer):
    _, n_exp, d, n = w.shape
    tk = min(d, 1024)
    blk = min(2 * LANES, n)

    def body(w_ref, o_ref):
        r = lax.broadcasted_iota(I32, (blk, blk), 0)
        c = lax.broadcasted_iota(I32, (blk, blk), 1)
        src = jnp.where(c < blk // 2, 2 * c, 2 * (c - blk // 2) + 1)
        perm = jnp.where(r == src, 1.0, 0.0).astype(BF16)
        for b in range(n // blk):
            cols = slice(b * blk, (b + 1) * blk)
            o_ref[0, :, cols] = _dot(w_ref[0, :, cols].astype(BF16), perm).astype(BF16)

    return pl.pallas_call(
        body,
        grid=(n_exp, d // tk),
        in_specs=[pl.BlockSpec((None, 1, tk, n), lambda e, k: (layer, e, k, 0))],
        out_specs=pl.BlockSpec((1, tk, n), lambda e, k: (e, k, 0)),
        out_shape=jax.ShapeDtypeStruct((n_exp, d, n), BF16),
        compiler_params=_cp(("parallel", "parallel"), 32),
        name="prep_wgu",
    )(w)


def _deinterleave_bias(b):
    n_exp, n = b.shape
    blk = min(2 * LANES, n)
    return b.reshape(n_exp, n // blk, blk // 2, 2).transpose(0, 1, 3, 2).reshape(n_exp, n)


def _experts(hp, inv, tab_e, tab_v, wgu, bgu, wdn, bdn, layer, n_tiles):
    n_exp, d, de2 = wgu.shape
    half = d // 2
    de = de2 // 2
    te = EXPERT_TILE
    rows_h, pitch_h = _token_rows(half)
    rows_y, pitch_y = _token_rows(d)
    blk = min(2 * LANES, de2)
    inv3 = inv.reshape(n_tiles, 1, te)

    def body(te_ref, tv_ref, cur_ref, nxt_ref, hp_ref, wgu_ref, bgu_ref, wdn_ref, bdn_ref,
             ys_ref, xbuf, sem):
        i = pl.program_id(0)
        slot = i % 2

        def row_copy(tok, sl, r):
            return pltpu.make_async_copy(hp_ref.at[pl.ds(tok * pitch_h, rows_h)],
                                         xbuf.at[sl, pl.ds(r * pitch_h, rows_h)], sem.at[sl])

        def issue(idx_ref, sl):
            def f(r, c):
                row_copy(idx_ref[0, 0, r], sl, r).start()
                return c

            lax.fori_loop(0, te, f, 0, unroll=8)

        @pl.when(jnp.logical_and(i == 0, tv_ref[0] > 0))
        def _():
            issue(cur_ref, 0)

        nxt = jnp.minimum(i + 1, n_tiles - 1)

        @pl.when(jnp.logical_and(i + 1 < n_tiles, tv_ref[nxt] > 0))
        def _():
            issue(nxt_ref, 1 - slot)

        @pl.when(tv_ref[i] > 0)
        def _():
            def w(r, c):
                row_copy(0, slot, 0).wait()
                return c

            lax.fori_loop(0, te, w, 0, unroll=8)
            los, his = [], []
            for cidx in range(rows_h):
                words = xbuf[slot, pl.ds(cidx, te, stride=pitch_h), :]
                los.append(pltpu.bitcast(lax.shift_left(words, jnp.uint32(16)), F32).astype(BF16))
                his.append(pltpu.bitcast(words & jnp.uint32(0xFFFF0000), F32).astype(BF16))
            lo = jnp.concatenate(los, axis=1)
            hi = jnp.concatenate(his, axis=1)
            gu = _dot(lo, wgu_ref[0, :half, :]) + _dot(hi, wgu_ref[0, half:, :]) + bgu_ref[0]
            acts = []
            for b in range(de2 // blk):
                gate = jnp.minimum(gu[:, b * blk:b * blk + blk // 2], SWIGLU_LIMIT)
                up = jnp.clip(gu[:, b * blk + blk // 2:(b + 1) * blk], -SWIGLU_LIMIT, SWIGLU_LIMIT)
                acts.append(((up + 1.0) * gate * jax.nn.sigmoid(SWIGLU_ALPHA * gate)).astype(BF16))
            out = _dot(jnp.concatenate(acts, axis=1), wdn_ref[0, 0]) + bdn_ref[0]
            for cidx in range(pitch_y):
                chunk = out[:, cidx * LANES:(cidx + 1) * LANES] if cidx < rows_y else jnp.zeros((te, LANES), F32)
                ys_ref[pl.ds(cidx, te, stride=pitch_y), :] = chunk

        @pl.when(tv_ref[i] == 0)
        def _():
            ys_ref[...] = jnp.zeros_like(ys_ref)

    grid_spec = pltpu.PrefetchScalarGridSpec(
        num_scalar_prefetch=2,
        grid=(n_tiles,),
        in_specs=[pl.BlockSpec((1, 1, te), lambda i, e, v: (i, 0, 0), memory_space=pltpu.SMEM),
                  pl.BlockSpec((1, 1, te), lambda i, e, v: (jnp.minimum(i + 1, n_tiles - 1), 0, 0),
                               memory_space=pltpu.SMEM),
                  pl.BlockSpec(memory_space=pl.ANY),
                  pl.BlockSpec((1, d, de2), lambda i, e, v: (e[i], 0, 0)),
                  pl.BlockSpec((1, 1, de2), lambda i, e, v: (e[i], 0, 0)),
                  pl.BlockSpec((1, 1, de, d), lambda i, e, v: (layer, e[i], 0, 0)),
                  pl.BlockSpec((1, 1, d), lambda i, e, v: (e[i], 0, 0))],
        out_specs=pl.BlockSpec((te * pitch_y, LANES), lambda i, e, v: (i, 0)),
        scratch_shapes=[pltpu.VMEM((2, te * pitch_h, LANES), U32), pltpu.SemaphoreType.DMA((2,))],
    )
    return pl.pallas_call(
        body,
        grid_spec=grid_spec,
        out_shape=jax.ShapeDtypeStruct((n_tiles * te * pitch_y, LANES), F32),
        compiler_params=_cp(("arbitrary",), 56),
        name="moe_experts",
    )(tab_e, tab_v, inv3, inv3, hp, wgu, bgu.reshape(n_exp, 1, de2), wdn, bdn.reshape(n_exp, 1, d))


def _combine(xs, ys, pos_flat, p4, mods, layer, dims, final_g):
    t, d = xs.shape
    tc = COMBINE_TILE
    nq = tc * TOP_K
    nct_rows = dims["B"] * dims["CTX"]
    nct, lt = nct_rows // tc, dims["L"] // tc
    final = final_g is not None
    skip = nct if final else 0
    n_steps = t // tc - skip
    pos3 = pos_flat.reshape(t // tc, 1, nq)
    rows_y, pitch_y = _token_rows(d)

    def body(cur_ref, nxt_ref, x_ref, p_ref, g_ref, fg_ref, ys_ref, o_ref, gbuf, sem):
        i = pl.program_id(0)
        slot = i % 2

        def row_copy(src_row, sl, dst_row):
            return pltpu.make_async_copy(ys_ref.at[pl.ds(src_row, rows_y)],
                                         gbuf.at[sl, pl.ds(dst_row, rows_y)], sem.at[sl])

        def issue(idx_ref, sl):
            def f(tok, c):
                for k in range(TOP_K):
                    row_copy(idx_ref[0, 0, tok * TOP_K + k] * pitch_y, sl, (k * tc + tok) * pitch_y).start()
                return c

            lax.fori_loop(0, tc, f, 0, unroll=2)

        @pl.when(i == 0)
        def _():
            issue(cur_ref, 0)

        @pl.when(i + 1 < n_steps)
        def _():
            issue(nxt_ref, 1 - slot)

        def w(q, c):
            row_copy(0, slot, 0).wait()
            return c

        lax.fori_loop(0, nq, w, 0, unroll=8)
        p = p_ref[...]
        pk = [p[:, k:k + 1] for k in range(TOP_K)]
        ys_cols = []
        for cidx in range(rows_y):
            y = pk[0] * gbuf[slot, pl.ds(cidx, tc, stride=pitch_y), :]
            for k in range(1, TOP_K):
                y = y + pk[k] * gbuf[slot, pl.ds(k * tc * pitch_y + cidx, tc, stride=pitch_y), :]
            ys_cols.append(y)
        xn = x_ref[...] + g_ref[0] * jnp.concatenate(ys_cols, axis=1)
        if final:
            xn = xn * lax.rsqrt(jnp.mean(xn * xn, axis=-1, keepdims=True) + EPS) * fg_ref[...]
        o_ref[...] = xn

    fg = jnp.ones((1, d), F32) if final_g is None else final_g.reshape(1, d)
    return pl.pallas_call(
        body,
        grid=(n_steps,),
        in_specs=[pl.BlockSpec((1, 1, nq), lambda i: (i + skip, 0, 0), memory_space=pltpu.SMEM),
                  pl.BlockSpec((1, 1, nq), lambda i: (jnp.minimum(i + 1, n_steps - 1) + skip, 0, 0),
                               memory_space=pltpu.SMEM),
                  pl.BlockSpec((tc, d), lambda i: (i + skip, 0)),
                  pl.BlockSpec((tc, LANES), lambda i: (i + skip, 0)),
                  pl.BlockSpec((1, 1, d), lambda i: (_mod_idx(layer, _seg(i + skip, nct, lt), 5), 0, 0)),
                  pl.BlockSpec((1, d), lambda i: (0, 0)),
                  pl.BlockSpec(memory_space=pl.ANY)],
        out_specs=pl.BlockSpec((tc, d), lambda i: (i, 0)),
        out_shape=jax.ShapeDtypeStruct((n_steps * tc, d), F32),
        scratch_shapes=[pltpu.VMEM((2, TOP_K * tc * pitch_y, LANES), F32), pltpu.SemaphoreType.DMA((2,))],
        compiler_params=_cp(("arbitrary",), 40),
        name="moe_combine",
    )(pos3, pos3, xs, p4, mods, fg, ys)


def kernel(x, c, ctx, c_ctx, norm1_g, norm2_g, w_ada, b_ada, w_in, conv_w, conv_b, gn_g, gn_b, lru_conv_w, lru_conv_b, lru_w_a, lru_b_a, lru_w_x, lru_b_x, lru_lambda, mix_norm_g, w_out, router_w, router_b, exp_w_gu, exp_b_gu, exp_w_down, exp_b_down, final_norm_g):
    nb, seq, d = x.shape
    ctx_len = ctx.shape[1]
    depth = w_in.shape[0]
    cw = conv_w.shape[2]
    n_exp = router_w.shape[2]
    de = exp_w_down.shape[2]
    tl = ctx_len
    t = nb * (ctx_len + seq)
    assert nb + 1 <= MOD_ROWS and seq % tl == 0 and n_exp <= LANES
    tm = min(512, nb * ctx_len)
    assert (nb * ctx_len) % tm == 0 and seq % tm == 0
    assert (nb * ctx_len) % COMBINE_TILE == 0 and seq % COMBINE_TILE == 0
    dims = {"B": nb, "CTX": ctx_len, "L": seq, "TL": tl, "NCT": nb, "LT": seq // tl, "TM": tm}
    n_tiles = (t * TOP_K) // EXPERT_TILE + n_exp
    n_slots = n_tiles * EXPERT_TILE

    cvec = jnp.concatenate([c_ctx[None], c, jnp.zeros((MOD_ROWS - 1 - nb, d), F32)], axis=0)
    mods = _adaln(cvec, w_ada, b_ada).reshape(depth * MOD_ROWS * N_MOD, 1, d)
    xs = jnp.concatenate([ctx.reshape(nb * ctx_len, d), x.reshape(nb * seq, d)], axis=0)
    w_in_b = w_in.astype(BF16)
    w_out_b = w_out.astype(BF16)
    w_down_b = exp_w_down.astype(BF16)

    for l in range(depth):
        last = l == depth - 1
        hl = _prenorm(xs, norm1_g[l], mods, l, dims)
        zl = _matmul(hl, w_in_b, l, tm, min(1024, w_in.shape[2]), "in_proj")
        mix_c = _conv_module(zl, conv_w[l], conv_b[l], gn_g[l], gn_b[l], mix_norm_g[l, :cw], dims)
        af, bf, ab, bb = _lru_gates(zl, lru_conv_w[l], lru_conv_b[l], lru_w_a[l], lru_b_a[l],
                                    lru_w_x[l], lru_b_x[l], lru_lambda[l], dims)
        hf, hb = _scan(af, bf, ab, bb, dims)
        mix_r = _lru_out(zl, hf, hb, mix_norm_g[l, cw:], dims)
        xs = _wout_residual(mix_c, mix_r, w_out_b, xs, mods, l, dims)

        rw = jnp.pad(router_w[l], ((0, 0), (0, LANES - n_exp)))
        rb = jnp.pad(router_b[l], (0, LANES - n_exp), constant_values=NEG_BIG).reshape(1, LANES)
        hp, e4, p4, mask, cnt = _router(xs, norm2_g[l], mods, l, rw, rb, dims)
        pos4, tab = _positions(mask, e4, cnt, n_exp, n_tiles, dims)
        pos_flat = pos4[:, :TOP_K].reshape(t * TOP_K)
        inv = _invert(pos_flat, n_slots)
        ys = _experts(hp, inv, tab[:n_tiles, 0], tab[:n_tiles, 1],
                      _prep_wgu(exp_w_gu, l), _deinterleave_bias(exp_b_gu[l]),
                      w_down_b, exp_b_down[l], l, n_tiles)
        xs = _combine(xs, ys, pos_flat, p4, mods, l, dims, final_norm_g if last else None)

    return xs.reshape(nb, seq, d)
```

```python
import jax
import jax.numpy as jnp
from jax import lax
from jax.experimental import pallas as pl
from jax.experimental.pallas import tpu as pltpu

GRID_W = 64
CONV_GROUPS = 16
TOP_K = 4
N_MOD = 6
EPS = 1e-6
LRU_C = 8.0
SWIGLU_LIMIT = 7.0
SWIGLU_ALPHA = 1.702

LANES = 128
SUBLANES = 8
MOD_ROWS = 8
HALO = 16
CONV_ROWS = 64
EXPERT_TILE = 256
COMBINE_TILE = 64
NEG_BIG = -1e30

F32 = jnp.float32
BF16 = jnp.bfloat16
U32 = jnp.uint32
I32 = jnp.int32


def _cp(sems, vmem_mb):
    return pltpu.CompilerParams(dimension_semantics=sems, vmem_limit_bytes=vmem_mb << 20)


def _seg(i, n_ctx_tiles, lat_tiles):
    return jnp.where(i < n_ctx_tiles, 0, 1 + (i - n_ctx_tiles) // lat_tiles)


def _mod_idx(layer, seg, j):
    return (layer * MOD_ROWS + seg) * N_MOD + j


def _dot(a, b):
    return jnp.dot(a, b, preferred_element_type=F32)


def _sigmoid(x):
    return 0.5 * jnp.tanh(0.5 * x) + 0.5


def _pack_words(x):
    half = x.shape[1] // 2
    bits = pltpu.bitcast(x.astype(BF16).astype(F32), U32)
    return (bits[:, half:] & jnp.uint32(0xFFFF0000)) | lax.shift_right_logical(bits[:, :half], jnp.uint32(16))


def _unpack_words(words):
    lo = pltpu.bitcast(lax.shift_left(words, jnp.uint32(16)), F32)
    hi = pltpu.bitcast(words & jnp.uint32(0xFFFF0000), F32)
    return lo, hi


def _token_rows(width):
    rows = width // LANES
    return rows, rows + 4


def _adaln(cvec, w_ada, b_ada):
    depth, d, n = w_ada.shape
    tn = min(n, 1024)

    def body(c_ref, w_ref, b_ref, o_ref):
        c = c_ref[...]
        s = (c * jax.nn.sigmoid(c)).astype(BF16)
        o_ref[0] = _dot(s, w_ref[0].astype(BF16)) + b_ref[0]

    return pl.pallas_call(
        body,
        grid=(depth, n // tn),
        in_specs=[pl.BlockSpec((MOD_ROWS, d), lambda l, j: (0, 0)),
                  pl.BlockSpec((1, d, tn), lambda l, j: (l, 0, j)),
                  pl.BlockSpec((1, 1, tn), lambda l, j: (l, 0, j))],
        out_specs=pl.BlockSpec((1, MOD_ROWS, tn), lambda l, j: (l, 0, j)),
        out_shape=jax.ShapeDtypeStruct((depth, MOD_ROWS, n), F32),
        compiler_params=_cp(("parallel", "parallel"), 48),
        name="adaln",
    )(cvec, w_ada, b_ada.reshape(depth, 1, n))


def _prenorm(xs, g, mods, layer, dims):
    t, d = xs.shape
    tl, nct, lt = dims["TL"], dims["NCT"], dims["LT"]

    def body(x_ref, g_ref, sh_ref, sc_ref, o_ref):
        x = x_ref[...]
        ms = jnp.mean(x * x, axis=-1, keepdims=True)
        y = x * lax.rsqrt(ms + EPS) * g_ref[...]
        o_ref[...] = (y * (1.0 + sc_ref[0]) + sh_ref[0]).astype(o_ref.dtype)

    return pl.pallas_call(
        body,
        grid=(t // tl,),
        in_specs=[pl.BlockSpec((tl, d), lambda i: (i, 0)),
                  pl.BlockSpec((1, d), lambda i: (0, 0)),
                  pl.BlockSpec((1, 1, d), lambda i: (_mod_idx(layer, _seg(i, nct, lt), 0), 0, 0)),
                  pl.BlockSpec((1, 1, d), lambda i: (_mod_idx(layer, _seg(i, nct, lt), 1), 0, 0))],
        out_specs=pl.BlockSpec((tl, d), lambda i: (i, 0)),
        out_shape=jax.ShapeDtypeStruct((t, d), BF16),
        compiler_params=_cp(("parallel",), 32),
        name="prenorm1",
    )(xs, g.reshape(1, d), mods, mods)


def _matmul(a, w, layer, tm, tn, name):
    m, k = a.shape
    n = w.shape[2]

    def body(a_ref, w_ref, o_ref):
        o_ref[...] = _dot(a_ref[...], w_ref[...])

    return pl.pallas_call(
        body,
        grid=(n // tn, m // tm),
        in_specs=[pl.BlockSpec((tm, k), lambda j, i: (i, 0)),
                  pl.BlockSpec((None, k, tn), lambda j, i: (layer, 0, j))],
        out_specs=pl.BlockSpec((tm, tn), lambda j, i: (i, j)),
        out_shape=jax.ShapeDtypeStruct((m, n), F32),
        compiler_params=_cp(("parallel", "parallel"), 48),
        name=name,
    )(a, w)


def _conv_module(zl, conv_w, conv_b, gn_g, gn_b, mix_g, dims):
    t = zl.shape[0]
    k_taps, cw = conv_w.shape
    pad = (k_taps - 1) // 2
    tl, nct, ctx_len = dims["TL"], dims["NCT"], dims["CTX"]
    assert cw // CONV_GROUPS == LANES and pad <= HALO
    assert tl % GRID_W == 0 and GRID_W % CONV_ROWS == 0 and ctx_len % CONV_ROWS == 0
    upad_rows = max((tl // GRID_W) * (GRID_W + 2 * HALO), ctx_len + 2 * HALO)

    def body(za_ref, zg_ref, w_ref, b_ref, gg_ref, gb_ref, mg_ref, o_ref, upad, ybuf, ssbuf):
        i = pl.program_id(0)
        upad[...] = jnp.zeros_like(upad)
        ssbuf[...] = jnp.zeros_like(ssbuf)

        def path(seg_len):
            nseg = tl // seg_len
            stride = seg_len + 2 * HALO

            def chunk(c, carry):
                ln = pl.ds(pl.multiple_of(c * LANES, LANES), LANES)
                u = za_ref[:, ln] * jax.nn.sigmoid(zg_ref[:, ln])
                for s in range(nseg):
                    upad[s * stride + HALO:s * stride + HALO + seg_len, :] = u[s * seg_len:(s + 1) * seg_len]
                bias = b_ref[:, ln]
                gg = gg_ref[:, ln]
                gb = gb_ref[:, ln]
                for s in range(nseg):
                    for r0 in range(0, seg_len, CONV_ROWS):
                        base = s * stride + HALO + r0 - pad
                        acc = jnp.broadcast_to(bias, (CONV_ROWS, LANES))
                        for k in range(k_taps):
                            acc = acc + w_ref[pl.ds(k, 1), ln] * upad[base + k:base + k + CONV_ROWS, :]
                        mu = jnp.mean(acc, axis=-1, keepdims=True)
                        dv = acc - mu
                        var = jnp.mean(dv * dv, axis=-1, keepdims=True)
                        y = dv * lax.rsqrt(var + EPS) * gg + gb
                        y = y * jax.nn.sigmoid(y)
                        rows = pl.ds(s * seg_len + r0, CONV_ROWS)
                        ybuf[rows, ln] = y
                        ssbuf[rows, :] = ssbuf[rows, :] + jnp.sum(y * y, axis=-1, keepdims=True)
                return carry

            lax.fori_loop(0, cw // LANES, chunk, 0)

        @pl.when(i < nct)
        def _():
            path(ctx_len)

        @pl.when(i >= nct)
        def _():
            path(GRID_W)

        inv = lax.rsqrt(ssbuf[:, 0:1] * (1.0 / cw) + EPS)
        o_ref[...] = (ybuf[...] * inv * mg_ref[...]).astype(o_ref.dtype)

    vec = pl.BlockSpec((1, cw), lambda i: (0, 0))
    return pl.pallas_call(
        body,
        grid=(t // tl,),
        in_specs=[pl.BlockSpec((tl, cw), lambda i: (i, 0)),
                  pl.BlockSpec((tl, cw), lambda i: (i, 1)),
                  pl.BlockSpec((k_taps, cw), lambda i: (0, 0)),
                  vec, vec, vec, vec],
        out_specs=pl.BlockSpec((tl, cw), lambda i: (i, 0)),
        out_shape=jax.ShapeDtypeStruct((t, cw), BF16),
        scratch_shapes=[pltpu.VMEM((upad_rows, LANES), F32),
                        pltpu.VMEM((tl, cw), F32),
                        pltpu.VMEM((tl, LANES), F32)],
        compiler_params=_cp(("parallel",), 40),
        name="conv_module",
    )(zl, zl, conv_w, conv_b.reshape(1, cw), gn_g.reshape(1, cw), gn_b.reshape(1, cw), mix_g.reshape(1, cw))


def _lru_gates(zl, cw, cb, wa, ba, wx, bx, lam, dims):
    t = zl.shape[0]
    ndir, kc, lw = cw.shape
    heads, hd = wa.shape[1], wa.shape[2]
    tl, nct, lt = dims["TL"], dims["NCT"], dims["LT"]
    assert hd == LANES and heads * hd == lw and ndir == 2 and kc - 1 <= SUBLANES
    nrb = t // SUBLANES
    tls = tl // SUBLANES

    def body(x_ref, p_ref, n_ref, cw_ref, cb_ref, wa_ref, ba_ref, wx_ref, bx_ref, lam_ref,
             af_ref, bf_ref, ab_ref, bb_ref, xp):
        i = pl.program_id(0)
        is_ctx = i < nct
        j = jnp.where(is_ctx, 0, (i - nct) % lt)
        first = jnp.logical_or(is_ctx, j == 0)
        last = jnp.logical_or(is_ctx, j == lt - 1)
        outs = ((af_ref, bf_ref), (ab_ref, bb_ref))

        def head(h, carry):
            ln = pl.ds(pl.multiple_of(h * LANES, LANES), LANES)
            xp[0:SUBLANES, :] = jnp.where(first, 0.0, p_ref[:, ln])
            xp[SUBLANES:SUBLANES + tl, :] = x_ref[:, ln]
            xp[SUBLANES + tl:2 * SUBLANES + tl, :] = jnp.where(last, 0.0, n_ref[:, ln])
            win = {o: xp[SUBLANES + o:SUBLANES + o + tl, :] for o in range(-(kc - 1), kc)}
            for d in range(ndir):
                xc = jnp.broadcast_to(cb_ref[pl.ds(d, 1), ln], (tl, LANES))
                for k in range(kc):
                    off = k - (kc - 1) if d == 0 else (kc - 1) - k
                    xc = xc + cw_ref[d, pl.ds(k, 1), ln] * win[off]
                xcb = xc.astype(BF16)
                r = _sigmoid(_dot(xcb, wa_ref[d, h].astype(BF16)) + ba_ref[pl.ds(d, 1), ln])
                ig = _sigmoid(_dot(xcb, wx_ref[d, h].astype(BF16)) + bx_ref[pl.ds(d, 1), ln])
                sp = jax.nn.softplus(-lam_ref[pl.ds(d, 1), ln])
                log_a = (-LRU_C * sp) * r
                a_ref, b_ref = outs[d]
                a = jnp.exp(log_a)
                a_ref[:, ln] = a
                b_ref[:, ln] = jnp.sqrt(1.0 - a * a) * (ig * xc)
            return carry

        lax.fori_loop(0, heads, head, 0)

    xcol = 2
    full2 = pl.BlockSpec((ndir, lw), lambda i: (0, 0))
    full4 = pl.BlockSpec((ndir, heads, hd, hd), lambda i: (0, 0, 0, 0))
    out = pl.BlockSpec((tl, lw), lambda i: (i, 0))
    shp = jax.ShapeDtypeStruct((t, lw), F32)
    return pl.pallas_call(
        body,
        grid=(t // tl,),
        in_specs=[pl.BlockSpec((tl, lw), lambda i: (i, xcol)),
                  pl.BlockSpec((SUBLANES, lw), lambda i: (jnp.maximum(i * tls - 1, 0), xcol)),
                  pl.BlockSpec((SUBLANES, lw), lambda i: (jnp.minimum((i + 1) * tls, nrb - 1), xcol)),
                  pl.BlockSpec((ndir, kc, lw), lambda i: (0, 0, 0)),
                  full2, full4, full2, full4, full2, full2],
        out_specs=[out, out, out, out],
        out_shape=[shp, shp, shp, shp],
        scratch_shapes=[pltpu.VMEM((tl + 2 * SUBLANES, LANES), F32)],
        compiler_params=_cp(("parallel",), 40),
        name="lru_gates",
    )(zl, zl, zl, cw, cb, wa, ba, wx, bx, lam)


def _scan(af, bf, ab, bb, dims):
    t, lw = af.shape
    hs = lw // LANES
    tl, nct, lt, nb = dims["TL"], dims["NCT"], dims["LT"], dims["B"]
    assert nct == nb

    def v(z):
        return z.reshape(t, hs, LANES)

    def fidx(b, s):
        return jnp.where(s == 0, b, nct + b * lt + s - 1)

    def bidx(b, s):
        return jnp.where(s == 0, b, nct + b * lt + lt - s)

    def body(af_ref, bf_ref, ab_ref, bb_ref, hf_ref, hb_ref, hc):
        @pl.when(pl.program_id(1) == 0)
        def _():
            hc[...] = jnp.zeros_like(hc)

        def block(q, carry):
            hf, hb = carry
            tf = pl.multiple_of(q * SUBLANES, SUBLANES)
            tb = pl.multiple_of(tl - SUBLANES - q * SUBLANES, SUBLANES)
            a_f, b_f = af_ref[pl.ds(tf, SUBLANES)], bf_ref[pl.ds(tf, SUBLANES)]
            a_b, b_b = ab_ref[pl.ds(tb, SUBLANES)], bb_ref[pl.ds(tb, SUBLANES)]
            of, ob = [], []
            for k in range(SUBLANES):
                hf = a_f[k] * hf + b_f[k]
                of.append(hf)
                kb = SUBLANES - 1 - k
                hb = a_b[kb] * hb + b_b[kb]
                ob.append(hb)
            hf_ref[pl.ds(tf, SUBLANES)] = jnp.stack(of)
            hb_ref[pl.ds(tb, SUBLANES)] = jnp.stack(ob[::-1])
            return hf, hb

        hf, hb = lax.fori_loop(0, tl // SUBLANES, block, (hc[0], hc[1]))
        hc[0] = hf
        hc[1] = hb

    fspec = pl.BlockSpec((tl, hs, LANES), lambda b, s: (fidx(b, s), 0, 0))
    bspec = pl.BlockSpec((tl, hs, LANES), lambda b, s: (bidx(b, s), 0, 0))
    shp = jax.ShapeDtypeStruct((t, hs, LANES), F32)
    hf, hb = pl.pallas_call(
        body,
        grid=(nb, 1 + lt),
        in_specs=[fspec, fspec, bspec, bspec],
        out_specs=[fspec, bspec],
        out_shape=[shp, shp],
        scratch_shapes=[pltpu.VMEM((2, hs, LANES), F32)],
        compiler_params=_cp(("parallel", "arbitrary"), 40),
        name="lru_scan",
    )(v(af), v(bf), v(ab), v(bb))
    return hf.reshape(t, lw), hb.reshape(t, lw)


def _lru_out(zl, hf, hb, mix_g, dims):
    t, lw = hf.shape
    tl = dims["TL"]
    gcol = 3

    def body(g_ref, hf_ref, hb_ref, mg_ref, o_ref):
        y = jax.nn.gelu(g_ref[...], approximate=True) * (hf_ref[...] + hb_ref[...])
        inv = lax.rsqrt(jnp.mean(y * y, axis=-1, keepdims=True) + EPS)
        o_ref[...] = (y * inv * mg_ref[...]).astype(o_ref.dtype)

    row = pl.BlockSpec((tl, lw), lambda i: (i, 0))
    return pl.pallas_call(
        body,
        grid=(t // tl,),
        in_specs=[pl.BlockSpec((tl, lw), lambda i: (i, gcol)), row, row,
                  pl.BlockSpec((1, lw), lambda i: (0, 0))],
        out_specs=row,
        out_shape=jax.ShapeDtypeStruct((t, lw), BF16),
        compiler_params=_cp(("parallel",), 40),
        name="lru_out",
    )(zl, hf, hb, mix_g.reshape(1, lw))


def _wout_residual(mix_c, mix_r, w_out, xs, mods, layer, dims):
    t, d = xs.shape
    kc = mix_c.shape[1]
    kr = mix_r.shape[1]
    assert kc == kr
    tm, tn = dims["TM"], min(d, 1024)
    nct, lt = dims["B"] * dims["CTX"] // tm, dims["L"] // tm

    def body(a1_ref, a2_ref, w1_ref, w2_ref, x_ref, g_ref, o_ref):
        acc = _dot(a1_ref[...], w1_ref[...]) + _dot(a2_ref[...], w2_ref[...])
        o_ref[...] = x_ref[...] + g_ref[0] * acc

    return pl.pallas_call(
        body,
        grid=(d // tn, t // tm),
        in_specs=[pl.BlockSpec((tm, kc), lambda j, i: (i, 0)),
                  pl.BlockSpec((tm, kr), lambda j, i: (i, 0)),
                  pl.BlockSpec((None, kc, tn), lambda j, i: (layer, 0, j)),
                  pl.BlockSpec((None, kr, tn), lambda j, i: (layer, 1, j)),
                  pl.BlockSpec((tm, tn), lambda j, i: (i, j)),
                  pl.BlockSpec((1, 1, tn), lambda j, i: (_mod_idx(layer, _seg(i, nct, lt), 2), 0, j))],
        out_specs=pl.BlockSpec((tm, tn), lambda j, i: (i, j)),
        out_shape=jax.ShapeDtypeStruct((t, d), F32),
        compiler_params=_cp(("parallel", "parallel"), 48),
        name="wout_residual",
    )(mix_c, mix_r, w_out, w_out, xs, mods)


def _router(xs, g, mods, layer, rw, rb, dims):
    t, d = xs.shape
    half = d // 2
    rows_h, pitch = _token_rows(half)
    tl, nct, lt = dims["TL"], dims["NCT"], dims["LT"]

    def body(x_ref, g_ref, sh_ref, sc_ref, rw_ref, rb_ref, hp_ref, e4_ref, p4_ref, mask_ref, cnt_ref):
        x = x_ref[...]
        ms = jnp.mean(x * x, axis=-1, keepdims=True)
        h = (x * lax.rsqrt(ms + EPS) * g_ref[...]) * (1.0 + sc_ref[0]) + sh_ref[0]
        hh = h.astype(BF16)
        hh32 = hh.astype(F32)
        words = _pack_words(h)
        for cidx in range(pitch):
            chunk = words[:, cidx * LANES:(cidx + 1) * LANES] if cidx < rows_h else jnp.zeros((tl, LANES), U32)
            hp_ref[pl.ds(cidx, tl, stride=pitch), :] = chunk

        w = rw_ref[...]
        wh = w.astype(BF16)
        wl = (w - wh.astype(F32)).astype(BF16)
        hl = (h - hh32).astype(BF16)
        logits = _dot(hh, wh) + (_dot(hh, wl) + _dot(hl, wh)) + rb_ref[...]

        lane = lax.broadcasted_iota(I32, (tl, LANES), 1)
        vals = logits
        tops, hots = [], []
        e4 = jnp.zeros((tl, LANES), I32)
        for k in range(TOP_K):
            m = jnp.max(vals, axis=-1, keepdims=True)
            sel = jnp.min(jnp.where(vals == m, lane, LANES), axis=-1, keepdims=True)
            hot = lane == sel
            tops.append(m)
            hots.append(hot)
            vals = jnp.where(hot, -jnp.inf, vals)
            e4 = jnp.where(lane == k, sel, e4)
        ex = [jnp.exp(m - tops[0]) for m in tops]
        den = ex[0]
        for e in ex[1:]:
            den = den + e
        p4 = jnp.zeros((tl, LANES), F32)
        mask = jnp.zeros((tl, LANES), F32)
        for k in range(TOP_K):
            p4 = jnp.where(lane == k, ex[k] / den, p4)
            mask = jnp.where(hots[k], 1.0, mask)
        e4_ref[...] = e4
        p4_ref[...] = p4
        mask_ref[...] = mask

        @pl.when(pl.program_id(0) == 0)
        def _():
            cnt_ref[...] = jnp.zeros_like(cnt_ref)

        cnt_ref[...] = cnt_ref[...] + jnp.sum(mask, axis=0, keepdims=True)

    lanes_out = pl.BlockSpec((tl, LANES), lambda i: (i, 0))
    return pl.pallas_call(
        body,
        grid=(t // tl,),
        in_specs=[pl.BlockSpec((tl, d), lambda i: (i, 0)),
                  pl.BlockSpec((1, d), lambda i: (0, 0)),
                  pl.BlockSpec((1, 1, d), lambda i: (_mod_idx(layer, _seg(i, nct, lt), 3), 0, 0)),
                  pl.BlockSpec((1, 1, d), lambda i: (_mod_idx(layer, _seg(i, nct, lt), 4), 0, 0)),
                  pl.BlockSpec((d, LANES), lambda i: (0, 0)),
                  pl.BlockSpec((1, LANES), lambda i: (0, 0))],
        out_specs=[pl.BlockSpec((tl * pitch, LANES), lambda i: (i, 0)), lanes_out, lanes_out, lanes_out,
                   pl.BlockSpec((SUBLANES, LANES), lambda i: (0, 0))],
        out_shape=[jax.ShapeDtypeStruct((t * pitch, LANES), U32),
                   jax.ShapeDtypeStruct((t, LANES), I32),
                   jax.ShapeDtypeStruct((t, LANES), F32),
                   jax.ShapeDtypeStruct((t, LANES), F32),
                   jax.ShapeDtypeStruct((SUBLANES, LANES), F32)],
        compiler_params=_cp(("arbitrary",), 40),
        name="router",
    )(xs, g.reshape(1, d), mods, mods, rw, rb)


def _positions(mask, e4, cnt, n_experts, n_tiles, dims):
    t = mask.shape[0]
    tl = dims["TL"]
    te = EXPERT_TILE
    ntp = -(-n_tiles // SUBLANES) * SUBLANES

    def lane_cumsum(x, lane):
        sh = 1
        while sh < LANES:
            x = x + jnp.where(lane >= sh, pltpu.roll(x, sh, axis=1), 0.0)
            sh *= 2
        return x

    def body(mask_ref, e4_ref, cnt_ref, pos_ref, tab_ref, carry):
        i = pl.program_id(0)
        lane8 = lax.broadcasted_iota(I32, (SUBLANES, LANES), 1)
        cnt = cnt_ref[...]
        gp = jnp.ceil(cnt * (1.0 / te)) * te
        inc = lane_cumsum(gp, lane8)
        off = inc - gp

        @pl.when(i == 0)
        def _():
            carry[...] = jnp.zeros_like(carry)
            lane = lax.broadcasted_iota(I32, (ntp, LANES), 1)
            tile = lax.broadcasted_iota(I32, (ntp, LANES), 0).astype(F32)
            total = jnp.max(inc[0:1], axis=-1, keepdims=True)
            start = tile * te
            used = start < total
            start_c = jnp.minimum(start, total - te)
            ex = jnp.sum(jnp.where(inc[0:1] <= start_c, 1.0, 0.0), axis=-1, keepdims=True)
            ex = jnp.minimum(ex, n_experts - 1.0)
            hot = lane.astype(F32) == ex
            cnt_e = jnp.sum(jnp.where(hot, cnt[0:1], 0.0), axis=-1, keepdims=True)
            off_e = jnp.sum(jnp.where(hot, off[0:1], 0.0), axis=-1, keepdims=True)
            valid = jnp.where(used, jnp.clip(cnt_e - (start - off_e), 0.0, te), 0.0)
            tab = jnp.where(lane == 0, ex, jnp.where(lane == 1, valid, 0.0))
            tab_ref[...] = tab.astype(I32)

        m = mask_ref[...]
        r = lax.broadcasted_iota(I32, (tl, tl), 0)
        c = lax.broadcasted_iota(I32, (tl, tl), 1)
        tri = jnp.where(r > c, 1.0, 0.0).astype(BF16)
        rank = _dot(tri, m.astype(BF16))
        posf = off[0:1] + carry[0:1] + rank
        lane = lax.broadcasted_iota(I32, (tl, LANES), 1)
        e4 = e4_ref[...]
        pos4 = jnp.zeros((tl, LANES), F32)
        for k in range(TOP_K):
            ek = jnp.sum(jnp.where(lane == k, e4, 0), axis=-1, keepdims=True)
            pk = jnp.sum(jnp.where(lane == ek, posf, 0.0), axis=-1, keepdims=True)
            pos4 = jnp.where(lane == k, pk, pos4)
        pos_ref[...] = pos4.astype(I32)
        carry[...] = carry[...] + jnp.sum(m, axis=0, keepdims=True)

    rows = pl.BlockSpec((tl, LANES), lambda i: (i, 0))
    return pl.pallas_call(
        body,
        grid=(t // tl,),
        in_specs=[rows, rows, pl.BlockSpec((SUBLANES, LANES), lambda i: (0, 0))],
        out_specs=[rows, pl.BlockSpec((ntp, LANES), lambda i: (0, 0))],
        out_shape=[jax.ShapeDtypeStruct((t, LANES), I32), jax.ShapeDtypeStruct((ntp, LANES), I32)],
        scratch_shapes=[pltpu.VMEM((SUBLANES, LANES), F32)],
        compiler_params=_cp(("arbitrary",), 32),
        name="moe_positions",
    )(mask, e4, cnt)


def _invert(pos_flat, n_slots):
    n = pos_flat.shape[0]

    def body(pos_ref, zeros_ref, inv_ref, sem):
        fill = pltpu.make_async_copy(zeros_ref, inv_ref, sem)
        fill.start()
        fill.wait()

        def put(q, c):
            inv_ref[pos_ref[q]] = lax.shift_right_logical(q, 2)
            return c

        lax.fori_loop(0, n, put, 0, unroll=8)

    assert TOP_K == 4
    return pl.pallas_call(
        body,
        in_specs=[pl.BlockSpec(memory_space=pltpu.SMEM), pl.BlockSpec(memory_space=pl.ANY)],
        out_specs=pl.BlockSpec(memory_space=pltpu.SMEM),
        out_shape=jax.ShapeDtypeStruct((n_slots,), I32),
        scratch_shapes=[pltpu.SemaphoreType.DMA(())],
        name="moe_invert",
    )(pos_flat, jnp.zeros((n_slots,), I32))


def _prep_wgu(w, layer):
    _, n_exp, d, n = w.shape
    tk = min(d, 2048)
    blk = min(2 * LANES, n)

    def body(w_ref, o_ref):
        r = lax.broadcasted_iota(I32, (blk, blk), 0)
        c = lax.broadcasted_iota(I32, (blk, blk), 1)
        src = jnp.where(c < blk // 2, 2 * c, 2 * (c - blk // 2) + 1)
        perm = jnp.where(r == src, 1.0, 0.0).astype(BF16)
        for b in range(n // blk):
            cols = slice(b * blk, (b + 1) * blk)
            o_ref[0, :, cols] = _dot(w_ref[0, :, cols].astype(BF16), perm).astype(BF16)

    return pl.pallas_call(
        body,
        grid=(n_exp, d // tk),
        in_specs=[pl.BlockSpec((None, 1, tk, n), lambda e, k: (layer, e, k, 0))],
        out_specs=pl.BlockSpec((1, tk, n), lambda e, k: (e, k, 0)),
        out_shape=jax.ShapeDtypeStruct((n_exp, d, n), BF16),
        compiler_params=_cp(("parallel", "parallel"), 40),
        name="prep_wgu",
    )(w)


def _deinterleave_bias(b):
    n_exp, n = b.shape
    blk = min(2 * LANES, n)
    return b.reshape(n_exp, n // blk, blk // 2, 2).transpose(0, 1, 3, 2).reshape(n_exp, n)


def _experts(hp, inv, tab_e, tab_v, wgu, bgu, wdn, bdn, layer, n_tiles):
    n_exp, d, de2 = wgu.shape
    half = d // 2
    de = de2 // 2
    te = EXPERT_TILE
    rows_h, pitch_h = _token_rows(half)
    blk = min(2 * LANES, de2)
    inv3 = inv.reshape(n_tiles, 1, te)

    def body(te_ref, tv_ref, cur_ref, nxt_ref, hp_ref, wgu_ref, bgu_ref, wdn_ref, bdn_ref,
             ys_ref, xbuf, wdn_b, sem):
        i = pl.program_id(0)
        slot = i % 2

        @pl.when(jnp.logical_or(i == 0, te_ref[i] != te_ref[jnp.maximum(i - 1, 0)]))
        def _():
            wdn_b[...] = wdn_ref[0, 0].astype(BF16)

        def row_copy(tok, sl, r):
            return pltpu.make_async_copy(hp_ref.at[pl.ds(tok * pitch_h, rows_h)],
                                         xbuf.at[sl, pl.ds(r * pitch_h, rows_h)], sem.at[sl])

        def issue(idx_ref, sl):
            def f(r, c):
                row_copy(idx_ref[0, 0, r], sl, r).start()
                return c

            lax.fori_loop(0, te, f, 0, unroll=8)

        @pl.when(jnp.logical_and(i == 0, tv_ref[0] > 0))
        def _():
            issue(cur_ref, 0)

        nxt = jnp.minimum(i + 1, n_tiles - 1)

        @pl.when(jnp.logical_and(i + 1 < n_tiles, tv_ref[nxt] > 0))
        def _():
            issue(nxt_ref, 1 - slot)

        @pl.when(tv_ref[i] > 0)
        def _():
            def w(r, c):
                row_copy(0, slot, 0).wait()
                return c

            lax.fori_loop(0, te, w, 0, unroll=8)
            los, his = [], []
            for cidx in range(rows_h):
                lo_c, hi_c = _unpack_words(xbuf[slot, pl.ds(cidx, te, stride=pitch_h), :])
                los.append(lo_c.astype(BF16))
                his.append(hi_c.astype(BF16))
            lo = jnp.concatenate(los, axis=1)
            hi = jnp.concatenate(his, axis=1)
            gu = _dot(lo, wgu_ref[0, :half, :]) + _dot(hi, wgu_ref[0, half:, :]) + bgu_ref[0]
            acts = []
            for b in range(de2 // blk):
                gate = jnp.minimum(gu[:, b * blk:b * blk + blk // 2], SWIGLU_LIMIT)
                up = jnp.clip(gu[:, b * blk + blk // 2:(b + 1) * blk], -SWIGLU_LIMIT, SWIGLU_LIMIT)
                acts.append(((up + 1.0) * gate * jax.nn.sigmoid(SWIGLU_ALPHA * gate)).astype(BF16))
            out = _pack_words(_dot(jnp.concatenate(acts, axis=1), wdn_b[...]) + bdn_ref[0])
            for cidx in range(pitch_h):
                chunk = out[:, cidx * LANES:(cidx + 1) * LANES] if cidx < rows_h else jnp.zeros((te, LANES), U32)
                ys_ref[pl.ds(cidx, te, stride=pitch_h), :] = chunk

        @pl.when(tv_ref[i] == 0)
        def _():
            ys_ref[...] = jnp.zeros_like(ys_ref)

    grid_spec = pltpu.PrefetchScalarGridSpec(
        num_scalar_prefetch=2,
        grid=(n_tiles,),
        in_specs=[pl.BlockSpec((1, 1, te), lambda i, e, v: (i, 0, 0), memory_space=pltpu.SMEM),
                  pl.BlockSpec((1, 1, te), lambda i, e, v: (jnp.minimum(i + 1, n_tiles - 1), 0, 0),
                               memory_space=pltpu.SMEM),
                  pl.BlockSpec(memory_space=pl.ANY),
                  pl.BlockSpec((1, d, de2), lambda i, e, v: (e[i], 0, 0)),
                  pl.BlockSpec((1, 1, de2), lambda i, e, v: (e[i], 0, 0)),
                  pl.BlockSpec((1, 1, de, d), lambda i, e, v: (layer, e[i], 0, 0)),
                  pl.BlockSpec((1, 1, d), lambda i, e, v: (e[i], 0, 0))],
        out_specs=pl.BlockSpec((te * pitch_h, LANES), lambda i, e, v: (i, 0)),
        scratch_shapes=[pltpu.VMEM((2, te * pitch_h, LANES), U32), pltpu.VMEM((de, d), BF16),
                        pltpu.SemaphoreType.DMA((2,))],
    )
    return pl.pallas_call(
        body,
        grid_spec=grid_spec,
        out_shape=jax.ShapeDtypeStruct((n_tiles * te * pitch_h, LANES), U32),
        compiler_params=_cp(("arbitrary",), 56),
        name="moe_experts",
    )(tab_e, tab_v, inv3, inv3, hp, wgu, bgu.reshape(n_exp, 1, de2), wdn, bdn.reshape(n_exp, 1, d))


def _combine(xs, ys, pos_flat, p4, mods, layer, dims, final_g, next_g):
    t, d = xs.shape
    tc = COMBINE_TILE
    nq = tc * TOP_K
    nct_rows = dims["B"] * dims["CTX"]
    nct, lt = nct_rows // tc, dims["L"] // tc
    final = final_g is not None
    skip = nct if final else 0
    n_steps = t // tc - skip
    pos3 = pos_flat.reshape(t // tc, 1, nq)
    rows_h, pitch_h = _token_rows(d // 2)

    def body(cur_ref, nxt_ref, x_ref, p_ref, g_ref, fg_ref, sh_ref, sc_ref, ys_ref, o_ref, *rest):
        h_ref = None if final else rest[0]
        gbuf, sem = rest[-2:]
        i = pl.program_id(0)
        slot = i % 2

        def row_copy(src_row, sl, dst_row):
            return pltpu.make_async_copy(ys_ref.at[pl.ds(src_row, rows_h)],
                                         gbuf.at[sl, pl.ds(dst_row, rows_h)], sem.at[sl])

        def issue(idx_ref, sl):
            def f(tok, c):
                for k in range(TOP_K):
                    row_copy(idx_ref[0, 0, tok * TOP_K + k] * pitch_h, sl, (k * tc + tok) * pitch_h).start()
                return c

            lax.fori_loop(0, tc, f, 0, unroll=2)

        @pl.when(i == 0)
        def _():
            issue(cur_ref, 0)

        @pl.when(i + 1 < n_steps)
        def _():
            issue(nxt_ref, 1 - slot)

        def w(q, c):
            row_copy(0, slot, 0).wait()
            return c

        lax.fori_loop(0, nq, w, 0, unroll=8)
        p = p_ref[...]
        pk = [p[:, k:k + 1] for k in range(TOP_K)]
        lo_cols, hi_cols = [], []
        for cidx in range(rows_h):
            y_lo = y_hi = None
            for k in range(TOP_K):
                lo, hi = _unpack_words(gbuf[slot, pl.ds(k * tc * pitch_h + cidx, tc, stride=pitch_h), :])
                y_lo = pk[k] * lo if k == 0 else y_lo + pk[k] * lo
                y_hi = pk[k] * hi if k == 0 else y_hi + pk[k] * hi
            lo_cols.append(y_lo)
            hi_cols.append(y_hi)
        xn = x_ref[...] + g_ref[0] * jnp.concatenate(lo_cols + hi_cols, axis=1)
        unit = xn * lax.rsqrt(jnp.mean(xn * xn, axis=-1, keepdims=True) + EPS)
        if final:
            o_ref[...] = unit * fg_ref[...]
        else:
            o_ref[...] = xn
            h_ref[...] = ((unit * fg_ref[...]) * (1.0 + sc_ref[0]) + sh_ref[0]).astype(h_ref.dtype)

    fg = (final_g if final else next_g).reshape(1, d)
    nl = layer if final else layer + 1
    row = pl.BlockSpec((tc, d), lambda i: (i, 0))
    shp = jax.ShapeDtypeStruct((n_steps * tc, d), F32)
    out = pl.pallas_call(
        body,
        grid=(n_steps,),
        in_specs=[pl.BlockSpec((1, 1, nq), lambda i: (i + skip, 0, 0), memory_space=pltpu.SMEM),
                  pl.BlockSpec((1, 1, nq), lambda i: (jnp.minimum(i + 1, n_steps - 1) + skip, 0, 0),
                               memory_space=pltpu.SMEM),
                  pl.BlockSpec((tc, d), lambda i: (i + skip, 0)),
                  pl.BlockSpec((tc, LANES), lambda i: (i + skip, 0)),
                  pl.BlockSpec((1, 1, d), lambda i: (_mod_idx(layer, _seg(i + skip, nct, lt), 5), 0, 0)),
                  pl.BlockSpec((1, d), lambda i: (0, 0)),
                  pl.BlockSpec((1, 1, d), lambda i: (_mod_idx(nl, _seg(i + skip, nct, lt), 0), 0, 0)),
                  pl.BlockSpec((1, 1, d), lambda i: (_mod_idx(nl, _seg(i + skip, nct, lt), 1), 0, 0)),
                  pl.BlockSpec(memory_space=pl.ANY)],
        out_specs=row if final else [row, row],
        out_shape=shp if final else [shp, jax.ShapeDtypeStruct((n_steps * tc, d), BF16)],
        scratch_shapes=[pltpu.VMEM((2, TOP_K * tc * pitch_h, LANES), U32), pltpu.SemaphoreType.DMA((2,))],
        compiler_params=_cp(("arbitrary",), 40),
        name="moe_combine",
    )(pos3, pos3, xs, p4, mods, fg, mods, mods, ys)
    return (out, None) if final else tuple(out)


def kernel(x, c, ctx, c_ctx, norm1_g, norm2_g, w_ada, b_ada, w_in, conv_w, conv_b, gn_g, gn_b, lru_conv_w, lru_conv_b, lru_w_a, lru_b_a, lru_w_x, lru_b_x, lru_lambda, mix_norm_g, w_out, router_w, router_b, exp_w_gu, exp_b_gu, exp_w_down, exp_b_down, final_norm_g):
    nb, seq, d = x.shape
    ctx_len = ctx.shape[1]
    depth = w_in.shape[0]
    cw = conv_w.shape[2]
    n_exp = router_w.shape[2]
    de = exp_w_down.shape[2]
    tl = ctx_len
    t = nb * (ctx_len + seq)
    assert nb + 1 <= MOD_ROWS and seq % tl == 0 and n_exp <= LANES
    tm = min(512, nb * ctx_len)
    assert (nb * ctx_len) % tm == 0 and seq % tm == 0
    assert (nb * ctx_len) % COMBINE_TILE == 0 and seq % COMBINE_TILE == 0
    dims = {"B": nb, "CTX": ctx_len, "L": seq, "TL": tl, "NCT": nb, "LT": seq // tl, "TM": tm}
    n_tiles = (t * TOP_K) // EXPERT_TILE + n_exp
    n_slots = n_tiles * EXPERT_TILE

    cvec = jnp.concatenate([c_ctx[None], c, jnp.zeros((MOD_ROWS - 1 - nb, d), F32)], axis=0)
    mods = _adaln(cvec, w_ada, b_ada).reshape(depth * MOD_ROWS * N_MOD, 1, d)
    xs = jnp.concatenate([ctx.reshape(nb * ctx_len, d), x.reshape(nb * seq, d)], axis=0)
    w_in_b = w_in.astype(BF16)
    w_out_b = w_out.astype(BF16)

    for l in range(depth):
        last = l == depth - 1
        if l == 0:
            hl = _prenorm(xs, norm1_g[l], mods, l, dims)
        zl = _matmul(hl, w_in_b, l, tm, min(1024, w_in.shape[2]), "in_proj")
        mix_c = _conv_module(zl, conv_w[l], conv_b[l], gn_g[l], gn_b[l], mix_norm_g[l, :cw], dims)
        af, bf, ab, bb = _lru_gates(zl, lru_conv_w[l], lru_conv_b[l], lru_w_a[l], lru_b_a[l],
                                    lru_w_x[l], lru_b_x[l], lru_lambda[l], dims)
        hf, hb = _scan(af, bf, ab, bb, dims)
        mix_r = _lru_out(zl, hf, hb, mix_norm_g[l, cw:], dims)
        xs = _wout_residual(mix_c, mix_r, w_out_b, xs, mods, l, dims)

        rw = jnp.pad(router_w[l], ((0, 0), (0, LANES - n_exp)))
        rb = jnp.pad(router_b[l], (0, LANES - n_exp), constant_values=NEG_BIG).reshape(1, LANES)
        hp, e4, p4, mask, cnt = _router(xs, norm2_g[l], mods, l, rw, rb, dims)
        pos4, tab = _positions(mask, e4, cnt, n_exp, n_tiles, dims)
        pos_flat = pos4[:, :TOP_K].reshape(t * TOP_K)
        inv = _invert(pos_flat, n_slots)
        ys = _experts(hp, inv, tab[:n_tiles, 0], tab[:n_tiles, 1],
                      _prep_wgu(exp_w_gu, l), _deinterleave_bias(exp_b_gu[l]),
                      exp_w_down, exp_b_down[l], l, n_tiles)
        xs, hl = _combine(xs, ys, pos_flat, p4, mods, l, dims, final_norm_g if last else None,
                          None if last else norm1_g[l + 1])

    return xs.reshape(nb, seq, d)
```

```python
import jax
import jax.numpy as jnp
from jax import lax
from jax.experimental import pallas as pl
from jax.experimental.pallas import tpu as pltpu

GRID_W = 64
CONV_GROUPS = 16
TOP_K = 4
N_MOD = 6
EPS = 1e-6
LRU_C = 8.0
SWIGLU_LIMIT = 7.0
SWIGLU_ALPHA = 1.702

LANES = 128
SUBLANES = 8
MOD_ROWS = 8
HALO = 16
CONV_ROWS = 64
EXPERT_TILE = 256
COMBINE_TILE = 64
NEG_BIG = -1e30

F32 = jnp.float32
BF16 = jnp.bfloat16
U32 = jnp.uint32
I32 = jnp.int32


def _cp(sems, vmem_mb):
    return pltpu.CompilerParams(dimension_semantics=sems, vmem_limit_bytes=vmem_mb << 20)


def _seg(i, n_ctx_tiles, lat_tiles):
    return jnp.where(i < n_ctx_tiles, 0, 1 + (i - n_ctx_tiles) // lat_tiles)


def _mod_idx(layer, seg, j):
    return (layer * MOD_ROWS + seg) * N_MOD + j


def _dot(a, b):
    return jnp.dot(a, b, preferred_element_type=F32)


def _sigmoid(x):
    return 0.5 * jnp.tanh(0.5 * x) + 0.5


def _pack_words(x):
    half = x.shape[1] // 2
    bits = pltpu.bitcast(x.astype(BF16).astype(F32), U32)
    return (bits[:, half:] & jnp.uint32(0xFFFF0000)) | lax.shift_right_logical(bits[:, :half], jnp.uint32(16))


def _unpack_words(words):
    lo = pltpu.bitcast(lax.shift_left(words, jnp.uint32(16)), F32)
    hi = pltpu.bitcast(words & jnp.uint32(0xFFFF0000), F32)
    return lo, hi


def _token_rows(width):
    rows = width // LANES
    return rows, rows + 4


def _adaln(cvec, w_ada, b_ada):
    depth, d, n = w_ada.shape
    tn = min(n, 1024)

    def body(c_ref, w_ref, b_ref, o_ref):
        c = c_ref[...]
        s = (c * jax.nn.sigmoid(c)).astype(BF16)
        o_ref[0] = _dot(s, w_ref[0].astype(BF16)) + b_ref[0]

    return pl.pallas_call(
        body,
        grid=(depth, n // tn),
        in_specs=[pl.BlockSpec((MOD_ROWS, d), lambda l, j: (0, 0)),
                  pl.BlockSpec((1, d, tn), lambda l, j: (l, 0, j)),
                  pl.BlockSpec((1, 1, tn), lambda l, j: (l, 0, j))],
        out_specs=pl.BlockSpec((1, MOD_ROWS, tn), lambda l, j: (l, 0, j)),
        out_shape=jax.ShapeDtypeStruct((depth, MOD_ROWS, n), F32),
        compiler_params=_cp(("parallel", "parallel"), 48),
        name="adaln",
    )(cvec, w_ada, b_ada.reshape(depth, 1, n))


def _prenorm(ctx_rows, lat_rows, g, mods, layer, dims):
    d = lat_rows.shape[1]
    t = ctx_rows.shape[0] + lat_rows.shape[0]
    tl, nct, lt = dims["TL"], dims["NCT"], dims["LT"]

    def body(c_ref, l_ref, g_ref, sh_ref, sc_ref, x_ref, o_ref):
        x = jnp.where(pl.program_id(0) < nct, c_ref[...], l_ref[...])
        x_ref[...] = x
        ms = jnp.mean(x * x, axis=-1, keepdims=True)
        y = x * lax.rsqrt(ms + EPS) * g_ref[...]
        o_ref[...] = (y * (1.0 + sc_ref[0]) + sh_ref[0]).astype(o_ref.dtype)

    row = pl.BlockSpec((tl, d), lambda i: (i, 0))
    return pl.pallas_call(
        body,
        grid=(t // tl,),
        in_specs=[pl.BlockSpec((tl, d), lambda i: (jnp.minimum(i, nct - 1), 0)),
                  pl.BlockSpec((tl, d), lambda i: (jnp.maximum(i - nct, 0), 0)),
                  pl.BlockSpec((1, d), lambda i: (0, 0)),
                  pl.BlockSpec((1, 1, d), lambda i: (_mod_idx(layer, _seg(i, nct, lt), 0), 0, 0)),
                  pl.BlockSpec((1, 1, d), lambda i: (_mod_idx(layer, _seg(i, nct, lt), 1), 0, 0))],
        out_specs=[row, row],
        out_shape=[jax.ShapeDtypeStruct((t, d), F32), jax.ShapeDtypeStruct((t, d), BF16)],
        compiler_params=_cp(("arbitrary",), 40),
        name="prenorm1",
    )(ctx_rows, lat_rows, g.reshape(1, d), mods, mods)


def _matmul(a, w, layer, tm, tn, name):
    m, k = a.shape
    n = w.shape[2]

    def body(a_ref, w_ref, o_ref):
        o_ref[...] = _dot(a_ref[...], w_ref[...])

    return pl.pallas_call(
        body,
        grid=(n // tn, m // tm),
        in_specs=[pl.BlockSpec((tm, k), lambda j, i: (i, 0)),
                  pl.BlockSpec((None, k, tn), lambda j, i: (layer, 0, j))],
        out_specs=pl.BlockSpec((tm, tn), lambda j, i: (i, j)),
        out_shape=jax.ShapeDtypeStruct((m, n), F32),
        compiler_params=_cp(("parallel", "parallel"), 48),
        name=name,
    )(a, w)


def _conv_module(zl, conv_w, conv_b, gn_g, gn_b, mix_g, dims):
    t = zl.shape[0]
    k_taps, cw = conv_w.shape
    pad = (k_taps - 1) // 2
    tl, nct, ctx_len = dims["TL"], dims["NCT"], dims["CTX"]
    assert cw // CONV_GROUPS == LANES and pad <= HALO
    assert tl % GRID_W == 0 and GRID_W % CONV_ROWS == 0 and ctx_len % CONV_ROWS == 0
    upad_rows = max((tl // GRID_W) * (GRID_W + 2 * HALO), ctx_len + 2 * HALO)

    def body(za_ref, zg_ref, w_ref, b_ref, gg_ref, gb_ref, mg_ref, o_ref, upad, ybuf, ssbuf):
        i = pl.program_id(0)
        upad[...] = jnp.zeros_like(upad)
        ssbuf[...] = jnp.zeros_like(ssbuf)

        def path(seg_len):
            nseg = tl // seg_len
            stride = seg_len + 2 * HALO

            def chunk(c, carry):
                ln = pl.ds(pl.multiple_of(c * LANES, LANES), LANES)
                u = za_ref[:, ln] * jax.nn.sigmoid(zg_ref[:, ln])
                for s in range(nseg):
                    upad[s * stride + HALO:s * stride + HALO + seg_len, :] = u[s * seg_len:(s + 1) * seg_len]
                bias = b_ref[:, ln]
                gg = gg_ref[:, ln]
                gb = gb_ref[:, ln]
                for s in range(nseg):
                    for r0 in range(0, seg_len, CONV_ROWS):
                        base = s * stride + HALO + r0 - pad
                        acc = jnp.broadcast_to(bias, (CONV_ROWS, LANES))
                        for k in range(k_taps):
                            acc = acc + w_ref[pl.ds(k, 1), ln] * upad[base + k:base + k + CONV_ROWS, :]
                        mu = jnp.mean(acc, axis=-1, keepdims=True)
                        dv = acc - mu
                        var = jnp.mean(dv * dv, axis=-1, keepdims=True)
                        y = dv * lax.rsqrt(var + EPS) * gg + gb
                        y = y * jax.nn.sigmoid(y)
                        rows = pl.ds(s * seg_len + r0, CONV_ROWS)
                        ybuf[rows, ln] = y
                        ssbuf[rows, :] = ssbuf[rows, :] + jnp.sum(y * y, axis=-1, keepdims=True)
                return carry

            lax.fori_loop(0, cw // LANES, chunk, 0)

        @pl.when(i < nct)
        def _():
            path(ctx_len)

        @pl.when(i >= nct)
        def _():
            path(GRID_W)

        inv = lax.rsqrt(ssbuf[:, 0:1] * (1.0 / cw) + EPS)
        o_ref[...] = (ybuf[...] * inv * mg_ref[...]).astype(o_ref.dtype)

    vec = pl.BlockSpec((1, cw), lambda i: (0, 0))
    return pl.pallas_call(
        body,
        grid=(t // tl,),
        in_specs=[pl.BlockSpec((tl, cw), lambda i: (i, 0)),
                  pl.BlockSpec((tl, cw), lambda i: (i, 1)),
                  pl.BlockSpec((k_taps, cw), lambda i: (0, 0)),
                  vec, vec, vec, vec],
        out_specs=pl.BlockSpec((tl, cw), lambda i: (i, 0)),
        out_shape=jax.ShapeDtypeStruct((t, cw), BF16),
        scratch_shapes=[pltpu.VMEM((upad_rows, LANES), F32),
                        pltpu.VMEM((tl, cw), F32),
                        pltpu.VMEM((tl, LANES), F32)],
        compiler_params=_cp(("parallel",), 40),
        name="conv_module",
    )(zl, zl, conv_w, conv_b.reshape(1, cw), gn_g.reshape(1, cw), gn_b.reshape(1, cw), mix_g.reshape(1, cw))


def _lru_gates(zl, cw, cb, wa, ba, wx, bx, lam, dims):
    t = zl.shape[0]
    ndir, kc, lw = cw.shape
    heads, hd = wa.shape[1], wa.shape[2]
    tl, nct, lt = dims["TL"], dims["NCT"], dims["LT"]
    assert hd == LANES and heads * hd == lw and ndir == 2 and kc - 1 <= SUBLANES
    nrb = t // SUBLANES
    tls = tl // SUBLANES

    def body(x_ref, p_ref, n_ref, cw_ref, cb_ref, wa_ref, ba_ref, wx_ref, bx_ref, lam_ref,
             af_ref, bf_ref, ab_ref, bb_ref, xp):
        i = pl.program_id(0)
        is_ctx = i < nct
        j = jnp.where(is_ctx, 0, (i - nct) % lt)
        first = jnp.logical_or(is_ctx, j == 0)
        last = jnp.logical_or(is_ctx, j == lt - 1)
        outs = ((af_ref, bf_ref), (ab_ref, bb_ref))

        def head(h, carry):
            ln = pl.ds(pl.multiple_of(h * LANES, LANES), LANES)
            xp[0:SUBLANES, :] = jnp.where(first, 0.0, p_ref[:, ln])
            xp[SUBLANES:SUBLANES + tl, :] = x_ref[:, ln]
            xp[SUBLANES + tl:2 * SUBLANES + tl, :] = jnp.where(last, 0.0, n_ref[:, ln])
            win = {o: xp[SUBLANES + o:SUBLANES + o + tl, :] for o in range(-(kc - 1), kc)}
            for d in range(ndir):
                xc = jnp.broadcast_to(cb_ref[pl.ds(d, 1), ln], (tl, LANES))
                for k in range(kc):
                    off = k - (kc - 1) if d == 0 else (kc - 1) - k
                    xc = xc + cw_ref[d, pl.ds(k, 1), ln] * win[off]
                xcb = xc.astype(BF16)
                r = _sigmoid(_dot(xcb, wa_ref[d, h].astype(BF16)) + ba_ref[pl.ds(d, 1), ln])
                ig = _sigmoid(_dot(xcb, wx_ref[d, h].astype(BF16)) + bx_ref[pl.ds(d, 1), ln])
                sp = jax.nn.softplus(-lam_ref[pl.ds(d, 1), ln])
                log_a = (-LRU_C * sp) * r
                a_ref, b_ref = outs[d]
                a = jnp.exp(log_a)
                a_ref[:, ln] = a
                b_ref[:, ln] = jnp.sqrt(1.0 - a * a) * (ig * xc)
            return carry

        lax.fori_loop(0, heads, head, 0)

    xcol = 2
    full2 = pl.BlockSpec((ndir, lw), lambda i: (0, 0))
    full4 = pl.BlockSpec((ndir, heads, hd, hd), lambda i: (0, 0, 0, 0))
    out = pl.BlockSpec((tl, lw), lambda i: (i, 0))
    shp = jax.ShapeDtypeStruct((t, lw), F32)
    return pl.pallas_call(
        body,
        grid=(t // tl,),
        in_specs=[pl.BlockSpec((tl, lw), lambda i: (i, xcol)),
                  pl.BlockSpec((SUBLANES, lw), lambda i: (jnp.maximum(i * tls - 1, 0), xcol)),
                  pl.BlockSpec((SUBLANES, lw), lambda i: (jnp.minimum((i + 1) * tls, nrb - 1), xcol)),
                  pl.BlockSpec((ndir, kc, lw), lambda i: (0, 0, 0)),
                  full2, full4, full2, full4, full2, full2],
        out_specs=[out, out, out, out],
        out_shape=[shp, shp, shp, shp],
        scratch_shapes=[pltpu.VMEM((tl + 2 * SUBLANES, LANES), F32)],
        compiler_params=_cp(("parallel",), 40),
        name="lru_gates",
    )(zl, zl, zl, cw, cb, wa, ba, wx, bx, lam)


def _scan(af, bf, ab, bb, dims):
    t, lw = af.shape
    hs = lw // LANES
    tl, nct, lt, nb = dims["TL"], dims["NCT"], dims["LT"], dims["B"]
    assert nct == nb

    def v(z):
        return z.reshape(t, hs, LANES)

    def fidx(b, s):
        return jnp.where(s == 0, b, nct + b * lt + s - 1)

    def bidx(b, s):
        return jnp.where(s == 0, b, nct + b * lt + lt - s)

    def body(af_ref, bf_ref, ab_ref, bb_ref, hf_ref, hb_ref, hc):
        @pl.when(pl.program_id(1) == 0)
        def _():
            hc[...] = jnp.zeros_like(hc)

        def block(q, carry):
            hf, hb = carry
            tf = pl.multiple_of(q * SUBLANES, SUBLANES)
            tb = pl.multiple_of(tl - SUBLANES - q * SUBLANES, SUBLANES)
            a_f, b_f = af_ref[pl.ds(tf, SUBLANES)], bf_ref[pl.ds(tf, SUBLANES)]
            a_b, b_b = ab_ref[pl.ds(tb, SUBLANES)], bb_ref[pl.ds(tb, SUBLANES)]
            of, ob = [], []
            for k in range(SUBLANES):
                hf = a_f[k] * hf + b_f[k]
                of.append(hf)
                kb = SUBLANES - 1 - k
                hb = a_b[kb] * hb + b_b[kb]
                ob.append(hb)
            hf_ref[pl.ds(tf, SUBLANES)] = jnp.stack(of)
            hb_ref[pl.ds(tb, SUBLANES)] = jnp.stack(ob[::-1])
            return hf, hb

        hf, hb = lax.fori_loop(0, tl // SUBLANES, block, (hc[0], hc[1]))
        hc[0] = hf
        hc[1] = hb

    fspec = pl.BlockSpec((tl, hs, LANES), lambda b, s: (fidx(b, s), 0, 0))
    bspec = pl.BlockSpec((tl, hs, LANES), lambda b, s: (bidx(b, s), 0, 0))
    shp = jax.ShapeDtypeStruct((t, hs, LANES), F32)
    hf, hb = pl.pallas_call(
        body,
        grid=(nb, 1 + lt),
        in_specs=[fspec, fspec, bspec, bspec],
        out_specs=[fspec, bspec],
        out_shape=[shp, shp],
        scratch_shapes=[pltpu.VMEM((2, hs, LANES), F32)],
        compiler_params=_cp(("parallel", "arbitrary"), 40),
        name="lru_scan",
    )(v(af), v(bf), v(ab), v(bb))
    return hf.reshape(t, lw), hb.reshape(t, lw)


def _lru_out(zl, hf, hb, mix_g, dims):
    t, lw = hf.shape
    tl = dims["TL"]
    gcol = 3

    def body(g_ref, hf_ref, hb_ref, mg_ref, o_ref):
        y = jax.nn.gelu(g_ref[...], approximate=True) * (hf_ref[...] + hb_ref[...])
        inv = lax.rsqrt(jnp.mean(y * y, axis=-1, keepdims=True) + EPS)
        o_ref[...] = (y * inv * mg_ref[...]).astype(o_ref.dtype)

    row = pl.BlockSpec((tl, lw), lambda i: (i, 0))
    return pl.pallas_call(
        body,
        grid=(t // tl,),
        in_specs=[pl.BlockSpec((tl, lw), lambda i: (i, gcol)), row, row,
                  pl.BlockSpec((1, lw), lambda i: (0, 0))],
        out_specs=row,
        out_shape=jax.ShapeDtypeStruct((t, lw), BF16),
        compiler_params=_cp(("parallel",), 40),
        name="lru_out",
    )(zl, hf, hb, mix_g.reshape(1, lw))


def _lru_pass(zl, cw, cb, wa, ba, wx, bx, lam, direction, dims, hf=None, mix_g=None):
    t = zl.shape[0]
    kc, lw = cw.shape
    heads, hd = wa.shape[0], wa.shape[1]
    tl, nct, lt, nb = dims["TL"], dims["NCT"], dims["LT"], dims["B"]
    assert hd == LANES and heads * hd == lw and kc - 1 <= SUBLANES and nct == nb
    nrb = t // SUBLANES
    tls = tl // SUBLANES
    _, pitch = _token_rows(lw)
    back = direction == 1
    xcol, gcol = 2, 3

    def tile(b, s):
        j = lt - s if back else s - 1
        return jnp.where(s == 0, b, nct + b * lt + j)

    def body(*refs):
        if back:
            (x_ref, p_ref, n_ref, cw_ref, cb_ref, wa_ref, ba_ref, wx_ref, bx_ref, lam_ref, g_ref, hf_ref, mg_ref,
             o_ref, xp, a_s, b_s, h_s, hc, ybuf, ssbuf) = refs
        else:
            (x_ref, p_ref, n_ref, cw_ref, cb_ref, wa_ref, ba_ref, wx_ref, bx_ref, lam_ref,
             o_ref, xp, a_s, b_s, h_s, hc) = refs
        s = pl.program_id(1)
        is_ctx = s == 0
        j = lt - s if back else s - 1
        first = jnp.logical_or(is_ctx, j == 0)
        last = jnp.logical_or(is_ctx, j == lt - 1)

        @pl.when(s == 0)
        def _():
            hc[...] = jnp.zeros_like(hc)

        def gates(h, carry):
            ln = pl.ds(pl.multiple_of(h * LANES, LANES), LANES)
            xp[0:SUBLANES, :] = jnp.where(first, 0.0, p_ref[:, ln])
            xp[SUBLANES:SUBLANES + tl, :] = x_ref[:, ln]
            xp[SUBLANES + tl:2 * SUBLANES + tl, :] = jnp.where(last, 0.0, n_ref[:, ln])
            xc = jnp.broadcast_to(cb_ref[:, ln], (tl, LANES))
            for k in range(kc):
                off = (kc - 1) - k if back else k - (kc - 1)
                xc = xc + cw_ref[pl.ds(k, 1), ln] * xp[SUBLANES + off:SUBLANES + off + tl, :]
            xcb = xc.astype(BF16)
            r = _sigmoid(_dot(xcb, wa_ref[h].astype(BF16)) + ba_ref[:, ln])
            ig = _sigmoid(_dot(xcb, wx_ref[h].astype(BF16)) + bx_ref[:, ln])
            log_a = (-LRU_C * jax.nn.softplus(-lam_ref[:, ln])) * r
            a = jnp.exp(log_a)
            a_s[pl.ds(h, tl, stride=pitch), :] = a
            b_s[pl.ds(h, tl, stride=pitch), :] = jnp.sqrt(1.0 - a * a) * (ig * xc)
            return carry

        lax.fori_loop(0, heads, gates, 0)

        def scan(q, hcur):
            for k in range(SUBLANES):
                step = q * SUBLANES + k
                row = ((tl - 1 - step) if back else step) * pitch
                hcur = a_s[pl.ds(row, heads), :] * hcur + b_s[pl.ds(row, heads), :]
                h_s[pl.ds(row, heads), :] = hcur
            return hcur

        hc[...] = lax.fori_loop(0, tl // SUBLANES, scan, hc[...])

        if back:
            ssbuf[...] = jnp.zeros_like(ssbuf)

        def finish(h, carry):
            ln = pl.ds(pl.multiple_of(h * LANES, LANES), LANES)
            hh = h_s[pl.ds(h, tl, stride=pitch), :]
            if back:
                y = jax.nn.gelu(g_ref[:, ln], approximate=True) * (hf_ref[:, ln] + hh)
                ybuf[:, ln] = y
                ssbuf[...] = ssbuf[...] + jnp.sum(y * y, axis=-1, keepdims=True)
            else:
                o_ref[:, ln] = hh
            return carry

        lax.fori_loop(0, heads, finish, 0)
        if back:
            inv = lax.rsqrt(ssbuf[:, 0:1] * (1.0 / lw) + EPS)
            o_ref[...] = (ybuf[...] * inv * mg_ref[...]).astype(o_ref.dtype)

    def col(c):
        return pl.BlockSpec((tl, lw), lambda b, s: (tile(b, s), c))

    vec = pl.BlockSpec((1, lw), lambda b, s: (0, 0))
    mat = pl.BlockSpec((heads, hd, hd), lambda b, s: (0, 0, 0))
    in_specs = [col(xcol),
                pl.BlockSpec((SUBLANES, lw), lambda b, s: (jnp.maximum(tile(b, s) * tls - 1, 0), xcol)),
                pl.BlockSpec((SUBLANES, lw), lambda b, s: (jnp.minimum((tile(b, s) + 1) * tls, nrb - 1), xcol)),
                pl.BlockSpec((kc, lw), lambda b, s: (0, 0)), vec, mat, vec, mat, vec, vec]
    args = [zl, zl, zl, cw, cb.reshape(1, lw), wa, ba.reshape(1, lw), wx, bx.reshape(1, lw), lam.reshape(1, lw)]
    scratch = [pltpu.VMEM((tl + 2 * SUBLANES, LANES), F32),
               pltpu.VMEM((tl * pitch, LANES), F32), pltpu.VMEM((tl * pitch, LANES), F32),
               pltpu.VMEM((tl * pitch, LANES), F32), pltpu.VMEM((heads, LANES), F32)]
    if back:
        in_specs += [col(gcol), col(0), vec]
        args += [zl, hf, mix_g.reshape(1, lw)]
        scratch += [pltpu.VMEM((tl, lw), F32), pltpu.VMEM((tl, LANES), F32)]
    return pl.pallas_call(
        body,
        grid=(nb, 1 + lt),
        in_specs=in_specs,
        out_specs=col(0),
        out_shape=jax.ShapeDtypeStruct((t, lw), BF16 if back else F32),
        scratch_shapes=scratch,
        compiler_params=_cp(("parallel", "arbitrary"), 48),
        name="lru_bwd" if back else "lru_fwd",
    )(*args)


def _wout_residual(mix_c, mix_r, w_out, xs, mods, layer, dims):
    t, d = xs.shape
    kc = mix_c.shape[1]
    kr = mix_r.shape[1]
    assert kc == kr
    tm, tn = dims["TM"], min(d, 1024)
    nct, lt = dims["B"] * dims["CTX"] // tm, dims["L"] // tm

    def body(a1_ref, a2_ref, w1_ref, w2_ref, x_ref, g_ref, o_ref):
        acc = _dot(a1_ref[...], w1_ref[...]) + _dot(a2_ref[...], w2_ref[...])
        o_ref[...] = x_ref[...] + g_ref[0] * acc

    return pl.pallas_call(
        body,
        grid=(d // tn, t // tm),
        in_specs=[pl.BlockSpec((tm, kc), lambda j, i: (i, 0)),
                  pl.BlockSpec((tm, kr), lambda j, i: (i, 0)),
                  pl.BlockSpec((None, kc, tn), lambda j, i: (layer, 0, j)),
                  pl.BlockSpec((None, kr, tn), lambda j, i: (layer, 1, j)),
                  pl.BlockSpec((tm, tn), lambda j, i: (i, j)),
                  pl.BlockSpec((1, 1, tn), lambda j, i: (_mod_idx(layer, _seg(i, nct, lt), 2), 0, j))],
        out_specs=pl.BlockSpec((tm, tn), lambda j, i: (i, j)),
        out_shape=jax.ShapeDtypeStruct((t, d), F32),
        compiler_params=_cp(("parallel", "parallel"), 48),
        name="wout_residual",
    )(mix_c, mix_r, w_out, w_out, xs, mods)


def _router(xs, g, mods, layer, rw, rb, dims):
    t, d = xs.shape
    half = d // 2
    rows_h, pitch = _token_rows(half)
    tl, nct, lt = dims["TL"], dims["NCT"], dims["LT"]

    def body(x_ref, g_ref, sh_ref, sc_ref, rw_ref, rb_ref, hp_ref, e4_ref, p4_ref, mask_ref, cnt_ref):
        x = x_ref[...]
        ms = jnp.mean(x * x, axis=-1, keepdims=True)
        h = (x * lax.rsqrt(ms + EPS) * g_ref[...]) * (1.0 + sc_ref[0]) + sh_ref[0]
        hh = h.astype(BF16)
        hh32 = hh.astype(F32)
        words = _pack_words(h)
        for cidx in range(pitch):
            chunk = words[:, cidx * LANES:(cidx + 1) * LANES] if cidx < rows_h else jnp.zeros((tl, LANES), U32)
            hp_ref[pl.ds(cidx, tl, stride=pitch), :] = chunk

        w = rw_ref[...]
        wh = w.astype(BF16)
        wl = (w - wh.astype(F32)).astype(BF16)
        hl = (h - hh32).astype(BF16)
        logits = _dot(hh, wh) + (_dot(hh, wl) + _dot(hl, wh)) + rb_ref[...]

        lane = lax.broadcasted_iota(I32, (tl, LANES), 1)
        vals = logits
        tops, hots = [], []
        e4 = jnp.zeros((tl, LANES), I32)
        for k in range(TOP_K):
            m = jnp.max(vals, axis=-1, keepdims=True)
            sel = jnp.min(jnp.where(vals == m, lane, LANES), axis=-1, keepdims=True)
            hot = lane == sel
            tops.append(m)
            hots.append(hot)
            vals = jnp.where(hot, -jnp.inf, vals)
            e4 = jnp.where(lane == k, sel, e4)
        ex = [jnp.exp(m - tops[0]) for m in tops]
        den = ex[0]
        for e in ex[1:]:
            den = den + e
        p4 = jnp.zeros((tl, LANES), F32)
        mask = jnp.zeros((tl, LANES), F32)
        for k in range(TOP_K):
            p4 = jnp.where(lane == k, ex[k] / den, p4)
            mask = jnp.where(hots[k], 1.0, mask)
        e4_ref[...] = e4
        p4_ref[...] = p4
        mask_ref[...] = mask

        @pl.when(pl.program_id(0) == 0)
        def _():
            cnt_ref[...] = jnp.zeros_like(cnt_ref)

        cnt_ref[...] = cnt_ref[...] + jnp.sum(mask, axis=0, keepdims=True)

    lanes_out = pl.BlockSpec((tl, LANES), lambda i: (i, 0))
    return pl.pallas_call(
        body,
        grid=(t // tl,),
        in_specs=[pl.BlockSpec((tl, d), lambda i: (i, 0)),
                  pl.BlockSpec((1, d), lambda i: (0, 0)),
                  pl.BlockSpec((1, 1, d), lambda i: (_mod_idx(layer, _seg(i, nct, lt), 3), 0, 0)),
                  pl.BlockSpec((1, 1, d), lambda i: (_mod_idx(layer, _seg(i, nct, lt), 4), 0, 0)),
                  pl.BlockSpec((d, LANES), lambda i: (0, 0)),
                  pl.BlockSpec((1, LANES), lambda i: (0, 0))],
        out_specs=[pl.BlockSpec((tl * pitch, LANES), lambda i: (i, 0)), lanes_out, lanes_out, lanes_out,
                   pl.BlockSpec((SUBLANES, LANES), lambda i: (0, 0))],
        out_shape=[jax.ShapeDtypeStruct((t * pitch, LANES), U32),
                   jax.ShapeDtypeStruct((t, LANES), I32),
                   jax.ShapeDtypeStruct((t, LANES), F32),
                   jax.ShapeDtypeStruct((t, LANES), F32),
                   jax.ShapeDtypeStruct((SUBLANES, LANES), F32)],
        compiler_params=_cp(("arbitrary",), 40),
        name="router",
    )(xs, g.reshape(1, d), mods, mods, rw, rb)


def _positions(mask, e4, cnt, n_experts, n_tiles, dims):
    t = mask.shape[0]
    tl = dims["TL"]
    te = EXPERT_TILE
    ntp = -(-n_tiles // SUBLANES) * SUBLANES

    def lane_cumsum(x, lane):
        sh = 1
        while sh < LANES:
            x = x + jnp.where(lane >= sh, pltpu.roll(x, sh, axis=1), 0.0)
            sh *= 2
        return x

    def body(mask_ref, e4_ref, cnt_ref, pos_ref, tab_ref, carry):
        i = pl.program_id(0)
        lane8 = lax.broadcasted_iota(I32, (SUBLANES, LANES), 1)
        cnt = cnt_ref[...]
        gp = jnp.ceil(cnt * (1.0 / te)) * te
        inc = lane_cumsum(gp, lane8)
        off = inc - gp

        @pl.when(i == 0)
        def _():
            carry[...] = jnp.zeros_like(carry)
            lane = lax.broadcasted_iota(I32, (ntp, LANES), 1)
            tile = lax.broadcasted_iota(I32, (ntp, LANES), 0).astype(F32)
            total = jnp.max(inc[0:1], axis=-1, keepdims=True)
            start = tile * te
            used = start < total
            start_c = jnp.minimum(start, total - te)
            ex = jnp.sum(jnp.where(inc[0:1] <= start_c, 1.0, 0.0), axis=-1, keepdims=True)
            ex = jnp.minimum(ex, n_experts - 1.0)
            hot = lane.astype(F32) == ex
            cnt_e = jnp.sum(jnp.where(hot, cnt[0:1], 0.0), axis=-1, keepdims=True)
            off_e = jnp.sum(jnp.where(hot, off[0:1], 0.0), axis=-1, keepdims=True)
            valid = jnp.where(used, jnp.clip(cnt_e - (start - off_e), 0.0, te), 0.0)
            tab = jnp.where(lane == 0, ex, jnp.where(lane == 1, valid, 0.0))
            tab_ref[...] = tab.astype(I32)

        m = mask_ref[...]
        r = lax.broadcasted_iota(I32, (tl, tl), 0)
        c = lax.broadcasted_iota(I32, (tl, tl), 1)
        tri = jnp.where(r > c, 1.0, 0.0).astype(BF16)
        rank = _dot(tri, m.astype(BF16))
        posf = off[0:1] + carry[0:1] + rank
        lane = lax.broadcasted_iota(I32, (tl, LANES), 1)
        e4 = e4_ref[...]
        pos4 = jnp.zeros((tl, LANES), F32)
        for k in range(TOP_K):
            ek = jnp.sum(jnp.where(lane == k, e4, 0), axis=-1, keepdims=True)
            pk = jnp.sum(jnp.where(lane == ek, posf, 0.0), axis=-1, keepdims=True)
            pos4 = jnp.where(lane == k, pk, pos4)
        pos_ref[...] = pos4.astype(I32)
        carry[...] = carry[...] + jnp.sum(m, axis=0, keepdims=True)

    rows = pl.BlockSpec((tl, LANES), lambda i: (i, 0))
    return pl.pallas_call(
        body,
        grid=(t // tl,),
        in_specs=[rows, rows, pl.BlockSpec((SUBLANES, LANES), lambda i: (0, 0))],
        out_specs=[rows, pl.BlockSpec((ntp, LANES), lambda i: (0, 0))],
        out_shape=[jax.ShapeDtypeStruct((t, LANES), I32), jax.ShapeDtypeStruct((ntp, LANES), I32)],
        scratch_shapes=[pltpu.VMEM((SUBLANES, LANES), F32)],
        compiler_params=_cp(("arbitrary",), 32),
        name="moe_positions",
    )(mask, e4, cnt)


def _invert(pos_flat, n_slots):
    n = pos_flat.shape[0]

    def body(pos_ref, zeros_ref, inv_ref, sem):
        fill = pltpu.make_async_copy(zeros_ref, inv_ref, sem)
        fill.start()
        fill.wait()

        def put(q, c):
            inv_ref[pos_ref[q]] = lax.shift_right_logical(q, 2)
            return c

        lax.fori_loop(0, n, put, 0, unroll=8)

    assert TOP_K == 4
    return pl.pallas_call(
        body,
        in_specs=[pl.BlockSpec(memory_space=pltpu.SMEM), pl.BlockSpec(memory_space=pl.ANY)],
        out_specs=pl.BlockSpec(memory_space=pltpu.SMEM),
        out_shape=jax.ShapeDtypeStruct((n_slots,), I32),
        scratch_shapes=[pltpu.SemaphoreType.DMA(())],
        name="moe_invert",
    )(pos_flat, jnp.zeros((n_slots,), I32))


def _prep_wgu(w, layer):
    _, n_exp, d, n = w.shape
    tk = min(d, 2048)
    blk = min(2 * LANES, n)

    def body(w_ref, o_ref):
        r = lax.broadcasted_iota(I32, (blk, blk), 0)
        c = lax.broadcasted_iota(I32, (blk, blk), 1)
        src = jnp.where(c < blk // 2, 2 * c, 2 * (c - blk // 2) + 1)
        perm = jnp.where(r == src, 1.0, 0.0).astype(BF16)
        for b in range(n // blk):
            cols = slice(b * blk, (b + 1) * blk)
            o_ref[0, :, cols] = _dot(w_ref[0, :, cols].astype(BF16), perm).astype(BF16)

    return pl.pallas_call(
        body,
        grid=(n_exp, d // tk),
        in_specs=[pl.BlockSpec((None, 1, tk, n), lambda e, k: (layer, e, k, 0))],
        out_specs=pl.BlockSpec((1, tk, n), lambda e, k: (e, k, 0)),
        out_shape=jax.ShapeDtypeStruct((n_exp, d, n), BF16),
        compiler_params=_cp(("parallel", "parallel"), 40),
        name="prep_wgu",
    )(w)


def _deinterleave_bias(b):
    n_exp, n = b.shape
    blk = min(2 * LANES, n)
    return b.reshape(n_exp, n // blk, blk // 2, 2).transpose(0, 1, 3, 2).reshape(n_exp, n)


def _experts(hp, inv, tab_e, tab_v, wgu, bgu, wdn, bdn, layer, n_tiles):
    n_exp, d, de2 = wgu.shape
    half = d // 2
    de = de2 // 2
    te = EXPERT_TILE
    rows_h, pitch_h = _token_rows(half)
    rows_y, pitch_y = _token_rows(d)
    blk = min(2 * LANES, de2)
    inv3 = inv.reshape(n_tiles, 1, te)

    def body(te_ref, tv_ref, cur_ref, nxt_ref, hp_ref, wgu_ref, bgu_ref, wdn_ref, bdn_ref,
             ys_ref, xbuf, wdn_b, sem):
        i = pl.program_id(0)
        slot = i % 2

        @pl.when(jnp.logical_or(i == 0, te_ref[i] != te_ref[jnp.maximum(i - 1, 0)]))
        def _():
            wdn_b[...] = wdn_ref[0, 0].astype(BF16)

        def row_copy(tok, sl, r):
            return pltpu.make_async_copy(hp_ref.at[pl.ds(tok * pitch_h, rows_h)],
                                         xbuf.at[sl, pl.ds(r * pitch_h, rows_h)], sem.at[sl])

        def issue(idx_ref, sl):
            def f(r, c):
                row_copy(idx_ref[0, 0, r], sl, r).start()
                return c

            lax.fori_loop(0, te, f, 0, unroll=8)

        @pl.when(jnp.logical_and(i == 0, tv_ref[0] > 0))
        def _():
            issue(cur_ref, 0)

        nxt = jnp.minimum(i + 1, n_tiles - 1)

        @pl.when(jnp.logical_and(i + 1 < n_tiles, tv_ref[nxt] > 0))
        def _():
            issue(nxt_ref, 1 - slot)

        @pl.when(tv_ref[i] > 0)
        def _():
            def w(r, c):
                row_copy(0, slot, 0).wait()
                return c

            lax.fori_loop(0, te, w, 0, unroll=8)
            los, his = [], []
            for cidx in range(rows_h):
                lo_c, hi_c = _unpack_words(xbuf[slot, pl.ds(cidx, te, stride=pitch_h), :])
                los.append(lo_c.astype(BF16))
                his.append(hi_c.astype(BF16))
            lo = jnp.concatenate(los, axis=1)
            hi = jnp.concatenate(his, axis=1)
            gu = _dot(lo, wgu_ref[0, :half, :]) + _dot(hi, wgu_ref[0, half:, :]) + bgu_ref[0]
            acts = []
            for b in range(de2 // blk):
                gate = jnp.minimum(gu[:, b * blk:b * blk + blk // 2], SWIGLU_LIMIT)
                up = jnp.clip(gu[:, b * blk + blk // 2:(b + 1) * blk], -SWIGLU_LIMIT, SWIGLU_LIMIT)
                acts.append(((up + 1.0) * gate * jax.nn.sigmoid(SWIGLU_ALPHA * gate)).astype(BF16))
            out = _dot(jnp.concatenate(acts, axis=1), wdn_b[...]) + bdn_ref[0]
            for cidx in range(pitch_y):
                chunk = out[:, cidx * LANES:(cidx + 1) * LANES] if cidx < rows_y else jnp.zeros((te, LANES), F32)
                ys_ref[pl.ds(cidx, te, stride=pitch_y), :] = chunk

        @pl.when(tv_ref[i] == 0)
        def _():
            ys_ref[...] = jnp.zeros_like(ys_ref)

    grid_spec = pltpu.PrefetchScalarGridSpec(
        num_scalar_prefetch=2,
        grid=(n_tiles,),
        in_specs=[pl.BlockSpec((1, 1, te), lambda i, e, v: (i, 0, 0), memory_space=pltpu.SMEM),
                  pl.BlockSpec((1, 1, te), lambda i, e, v: (jnp.minimum(i + 1, n_tiles - 1), 0, 0),
                               memory_space=pltpu.SMEM),
                  pl.BlockSpec(memory_space=pl.ANY),
                  pl.BlockSpec((1, d, de2), lambda i, e, v: (e[i], 0, 0)),
                  pl.BlockSpec((1, 1, de2), lambda i, e, v: (e[i], 0, 0)),
                  pl.BlockSpec((1, 1, de, d), lambda i, e, v: (layer, e[i], 0, 0)),
                  pl.BlockSpec((1, 1, d), lambda i, e, v: (e[i], 0, 0))],
        out_specs=pl.BlockSpec((te * pitch_y, LANES), lambda i, e, v: (i, 0)),
        scratch_shapes=[pltpu.VMEM((2, te * pitch_h, LANES), U32), pltpu.VMEM((de, d), BF16),
                        pltpu.SemaphoreType.DMA((2,))],
    )
    return pl.pallas_call(
        body,
        grid_spec=grid_spec,
        out_shape=jax.ShapeDtypeStruct((n_tiles * te * pitch_y, LANES), F32),
        compiler_params=_cp(("arbitrary",), 56),
        name="moe_experts",
    )(tab_e, tab_v, inv3, inv3, hp, wgu, bgu.reshape(n_exp, 1, de2), wdn, bdn.reshape(n_exp, 1, d))


def _combine(xs, ys, pos_flat, p4, mods, layer, dims, final_g, next_g):
    t, d = xs.shape
    tc = COMBINE_TILE
    nq = tc * TOP_K
    nct_rows = dims["B"] * dims["CTX"]
    nct, lt = nct_rows // tc, dims["L"] // tc
    final = final_g is not None
    skip = nct if final else 0
    n_steps = t // tc - skip
    pos3 = pos_flat.reshape(t // tc, 1, nq)
    rows_h, pitch_h = _token_rows(d)

    def body(cur_ref, nxt_ref, x_ref, p_ref, g_ref, fg_ref, sh_ref, sc_ref, ys_ref, o_ref, *rest):
        h_ref = None if final else rest[0]
        gbuf, sem = rest[-2:]
        i = pl.program_id(0)
        slot = i % 2

        def row_copy(src_row, sl, dst_row):
            return pltpu.make_async_copy(ys_ref.at[pl.ds(src_row, rows_h)],
                                         gbuf.at[sl, pl.ds(dst_row, rows_h)], sem.at[sl])

        def issue(idx_ref, sl):
            def f(tok, c):
                for k in range(TOP_K):
                    row_copy(idx_ref[0, 0, tok * TOP_K + k] * pitch_h, sl, (k * tc + tok) * pitch_h).start()
                return c

            lax.fori_loop(0, tc, f, 0, unroll=2)

        @pl.when(i == 0)
        def _():
            issue(cur_ref, 0)

        @pl.when(i + 1 < n_steps)
        def _():
            issue(nxt_ref, 1 - slot)

        def w(q, c):
            row_copy(0, slot, 0).wait()
            return c

        lax.fori_loop(0, nq, w, 0, unroll=8)
        p = p_ref[...]
        pk = [p[:, k:k + 1] for k in range(TOP_K)]
        ys_cols = []
        for cidx in range(rows_h):
            y = pk[0] * gbuf[slot, pl.ds(cidx, tc, stride=pitch_h), :]
            for k in range(1, TOP_K):
                y = y + pk[k] * gbuf[slot, pl.ds(k * tc * pitch_h + cidx, tc, stride=pitch_h), :]
            ys_cols.append(y)
        xn = x_ref[...] + g_ref[0] * jnp.concatenate(ys_cols, axis=1)
        unit = xn * lax.rsqrt(jnp.mean(xn * xn, axis=-1, keepdims=True) + EPS)
        if final:
            o_ref[...] = unit * fg_ref[...]
        else:
            o_ref[...] = xn
            h_ref[...] = ((unit * fg_ref[...]) * (1.0 + sc_ref[0]) + sh_ref[0]).astype(h_ref.dtype)

    fg = (final_g if final else next_g).reshape(1, d)
    nl = layer if final else layer + 1
    row = pl.BlockSpec((tc, d), lambda i: (i, 0))
    shp = jax.ShapeDtypeStruct((n_steps * tc, d), F32)
    out = pl.pallas_call(
        body,
        grid=(n_steps,),
        in_specs=[pl.BlockSpec((1, 1, nq), lambda i: (i + skip, 0, 0), memory_space=pltpu.SMEM),
                  pl.BlockSpec((1, 1, nq), lambda i: (jnp.minimum(i + 1, n_steps - 1) + skip, 0, 0),
                               memory_space=pltpu.SMEM),
                  pl.BlockSpec((tc, d), lambda i: (i + skip, 0)),
                  pl.BlockSpec((tc, LANES), lambda i: (i + skip, 0)),
                  pl.BlockSpec((1, 1, d), lambda i: (_mod_idx(layer, _seg(i + skip, nct, lt), 5), 0, 0)),
                  pl.BlockSpec((1, d), lambda i: (0, 0)),
                  pl.BlockSpec((1, 1, d), lambda i: (_mod_idx(nl, _seg(i + skip, nct, lt), 0), 0, 0)),
                  pl.BlockSpec((1, 1, d), lambda i: (_mod_idx(nl, _seg(i + skip, nct, lt), 1), 0, 0)),
                  pl.BlockSpec(memory_space=pl.ANY)],
        out_specs=row if final else [row, row],
        out_shape=shp if final else [shp, jax.ShapeDtypeStruct((n_steps * tc, d), BF16)],
        scratch_shapes=[pltpu.VMEM((2, TOP_K * tc * pitch_h, LANES), F32), pltpu.SemaphoreType.DMA((2,))],
        compiler_params=_cp(("arbitrary",), 40),
        name="moe_combine",
    )(pos3, pos3, xs, p4, mods, fg, mods, mods, ys)
    return (out, None) if final else tuple(out)


def kernel(x, c, ctx, c_ctx, norm1_g, norm2_g, w_ada, b_ada, w_in, conv_w, conv_b, gn_g, gn_b, lru_conv_w, lru_conv_b, lru_w_a, lru_b_a, lru_w_x, lru_b_x, lru_lambda, mix_norm_g, w_out, router_w, router_b, exp_w_gu, exp_b_gu, exp_w_down, exp_b_down, final_norm_g):
    nb, seq, d = x.shape
    ctx_len = ctx.shape[1]
    depth = w_in.shape[0]
    cw = conv_w.shape[2]
    n_exp = router_w.shape[2]
    de = exp_w_down.shape[2]
    tl = ctx_len
    t = nb * (ctx_len + seq)
    assert nb + 1 <= MOD_ROWS and seq % tl == 0 and n_exp <= LANES
    tm = min(512, nb * ctx_len)
    assert (nb * ctx_len) % tm == 0 and seq % tm == 0
    assert (nb * ctx_len) % COMBINE_TILE == 0 and seq % COMBINE_TILE == 0
    dims = {"B": nb, "CTX": ctx_len, "L": seq, "TL": tl, "NCT": nb, "LT": seq // tl, "TM": tm}
    n_tiles = (t * TOP_K) // EXPERT_TILE + n_exp
    n_slots = n_tiles * EXPERT_TILE

    cvec = jnp.concatenate([c_ctx[None], c, jnp.zeros((MOD_ROWS - 1 - nb, d), F32)], axis=0)
    mods = _adaln(cvec, w_ada, b_ada).reshape(depth * MOD_ROWS * N_MOD, 1, d)
    xs, hl = _prenorm(ctx.reshape(nb * ctx_len, d), x.reshape(nb * seq, d), norm1_g[0], mods, 0, dims)
    w_in_b = w_in.astype(BF16)
    w_out_b = w_out.astype(BF16)

    for l in range(depth):
        last = l == depth - 1
        zl = _matmul(hl, w_in_b, l, tm, min(1024, w_in.shape[2]), "in_proj")
        mix_c = _conv_module(zl, conv_w[l], conv_b[l], gn_g[l], gn_b[l], mix_norm_g[l, :cw], dims)
        lru = [(lru_conv_w[l, k], lru_conv_b[l, k], lru_w_a[l, k], lru_b_a[l, k],
                lru_w_x[l, k], lru_b_x[l, k], lru_lambda[l, k]) for k in range(2)]
        hf = _lru_pass(zl, *lru[0], 0, dims)
        mix_r = _lru_pass(zl, *lru[1], 1, dims, hf=hf, mix_g=mix_norm_g[l, cw:])
        xs = _wout_residual(mix_c, mix_r, w_out_b, xs, mods, l, dims)

        rw = jnp.pad(router_w[l], ((0, 0), (0, LANES - n_exp)))
        rb = jnp.pad(router_b[l], (0, LANES - n_exp), constant_values=NEG_BIG).reshape(1, LANES)
        hp, e4, p4, mask, cnt = _router(xs, norm2_g[l], mods, l, rw, rb, dims)
        pos4, tab = _positions(mask, e4, cnt, n_exp, n_tiles, dims)
        pos_flat = pos4[:, :TOP_K].reshape(t * TOP_K)
        inv = _invert(pos_flat, n_slots)
        ys = _experts(hp, inv, tab[:n_tiles, 0], tab[:n_tiles, 1],
                      _prep_wgu(exp_w_gu, l), _deinterleave_bias(exp_b_gu[l]),
                      exp_w_down, exp_b_down[l], l, n_tiles)
        xs, hl = _combine(xs, ys, pos_flat, p4, mods, l, dims, final_norm_g if last else None,
                          None if last else norm1_g[l + 1])

    return xs.reshape(nb, seq, d)
```

```python
import jax
import jax.numpy as jnp
from jax import lax
from jax.experimental import pallas as pl
from jax.experimental.pallas import tpu as pltpu

GRID_W = 64
CONV_GROUPS = 16
TOP_K = 4
N_MOD = 6
EPS = 1e-6
LRU_C = 8.0
SWIGLU_LIMIT = 7.0
SWIGLU_ALPHA = 1.702

LANES = 128
SUBLANES = 8
MOD_ROWS = 8
HALO = 16
CONV_ROWS = 64
EXPERT_TILE = 256
COMBINE_TILE = 64
NEG_BIG = -1e30

F32 = jnp.float32
BF16 = jnp.bfloat16
U32 = jnp.uint32
I32 = jnp.int32


def _cp(sems, vmem_mb):
    return pltpu.CompilerParams(dimension_semantics=sems, vmem_limit_bytes=vmem_mb << 20)


def _seg(i, n_ctx_tiles, lat_tiles):
    return jnp.where(i < n_ctx_tiles, 0, 1 + (i - n_ctx_tiles) // lat_tiles)


def _mod_idx(layer, seg, j):
    return (layer * MOD_ROWS + seg) * N_MOD + j


def _dot(a, b):
    return jnp.dot(a, b, preferred_element_type=F32)


def _sigmoid(x):
    return 0.5 * jnp.tanh(0.5 * x) + 0.5


def _pack_words(x):
    half = x.shape[1] // 2
    bits = pltpu.bitcast(x.astype(BF16).astype(F32), U32)
    return (bits[:, half:] & jnp.uint32(0xFFFF0000)) | lax.shift_right_logical(bits[:, :half], jnp.uint32(16))


def _unpack_words(words):
    lo = pltpu.bitcast(lax.shift_left(words, jnp.uint32(16)), F32)
    hi = pltpu.bitcast(words & jnp.uint32(0xFFFF0000), F32)
    return lo, hi


def _token_rows(width):
    rows = width // LANES
    return rows, rows + 4


def _adaln(cvec, w_ada, b_ada):
    depth, d, n = w_ada.shape
    tn = min(n, 1024)

    def body(c_ref, w_ref, b_ref, o_ref):
        c = c_ref[...]
        s = (c * jax.nn.sigmoid(c)).astype(BF16)
        o_ref[0] = _dot(s, w_ref[0].astype(BF16)) + b_ref[0]

    return pl.pallas_call(
        body,
        grid=(depth, n // tn),
        in_specs=[pl.BlockSpec((MOD_ROWS, d), lambda l, j: (0, 0)),
                  pl.BlockSpec((1, d, tn), lambda l, j: (l, 0, j)),
                  pl.BlockSpec((1, 1, tn), lambda l, j: (l, 0, j))],
        out_specs=pl.BlockSpec((1, MOD_ROWS, tn), lambda l, j: (l, 0, j)),
        out_shape=jax.ShapeDtypeStruct((depth, MOD_ROWS, n), F32),
        compiler_params=_cp(("parallel", "parallel"), 48),
        name="adaln",
    )(cvec, w_ada, b_ada.reshape(depth, 1, n))


def _prenorm(ctx_rows, lat_rows, g, mods, layer, dims):
    d = lat_rows.shape[1]
    t = ctx_rows.shape[0] + lat_rows.shape[0]
    tl, nct, lt = dims["TL"], dims["NCT"], dims["LT"]

    def body(c_ref, l_ref, g_ref, sh_ref, sc_ref, x_ref, o_ref):
        x = jnp.where(pl.program_id(0) < nct, c_ref[...], l_ref[...])
        x_ref[...] = x
        ms = jnp.mean(x * x, axis=-1, keepdims=True)
        y = x * lax.rsqrt(ms + EPS) * g_ref[...]
        o_ref[...] = (y * (1.0 + sc_ref[0]) + sh_ref[0]).astype(o_ref.dtype)

    row = pl.BlockSpec((tl, d), lambda i: (i, 0))
    return pl.pallas_call(
        body,
        grid=(t // tl,),
        in_specs=[pl.BlockSpec((tl, d), lambda i: (jnp.minimum(i, nct - 1), 0)),
                  pl.BlockSpec((tl, d), lambda i: (jnp.maximum(i - nct, 0), 0)),
                  pl.BlockSpec((1, d), lambda i: (0, 0)),
                  pl.BlockSpec((1, 1, d), lambda i: (_mod_idx(layer, _seg(i, nct, lt), 0), 0, 0)),
                  pl.BlockSpec((1, 1, d), lambda i: (_mod_idx(layer, _seg(i, nct, lt), 1), 0, 0))],
        out_specs=[row, row],
        out_shape=[jax.ShapeDtypeStruct((t, d), F32), jax.ShapeDtypeStruct((t, d), BF16)],
        compiler_params=_cp(("arbitrary",), 40),
        name="prenorm1",
    )(ctx_rows, lat_rows, g.reshape(1, d), mods, mods)


def _matmul(a, w, layer, tm, tn, name):
    m, k = a.shape
    n = w.shape[2]

    def body(a_ref, w_ref, o_ref):
        o_ref[...] = _dot(a_ref[...], w_ref[...])

    return pl.pallas_call(
        body,
        grid=(n // tn, m // tm),
        in_specs=[pl.BlockSpec((tm, k), lambda j, i: (i, 0)),
                  pl.BlockSpec((None, k, tn), lambda j, i: (layer, 0, j))],
        out_specs=pl.BlockSpec((tm, tn), lambda j, i: (i, j)),
        out_shape=jax.ShapeDtypeStruct((m, n), F32),
        compiler_params=_cp(("parallel", "parallel"), 48),
        name=name,
    )(a, w)


def _conv_module(zl, conv_w, conv_b, gn_g, gn_b, mix_g, dims):
    t = zl.shape[0]
    k_taps, cw = conv_w.shape
    pad = (k_taps - 1) // 2
    tl, nct, ctx_len = dims["TL"], dims["NCT"], dims["CTX"]
    assert cw // CONV_GROUPS == LANES and pad <= HALO
    assert tl % GRID_W == 0 and GRID_W % CONV_ROWS == 0 and ctx_len % CONV_ROWS == 0
    upad_rows = max((tl // GRID_W) * (GRID_W + 2 * HALO), ctx_len + 2 * HALO)

    assert (cw // LANES) % 2 == 0

    def body(za_ref, zg_ref, w_ref, b_ref, gg_ref, gb_ref, mg_ref, o_ref, upad, ybuf, ssbuf, vb0, vb1):
        i = pl.program_id(0)
        upad[...] = jnp.zeros_like(upad)
        ssbuf[...] = jnp.zeros_like(ssbuf)
        def path(seg_len):
            nseg = tl // seg_len
            stride = seg_len + 2 * HALO
            nchunk = cw // LANES

            def lanes(c):
                return pl.ds(pl.multiple_of(c * LANES, LANES), LANES)

            def conv(c, vbuf):
                ln = lanes(c)
                u = za_ref[:, ln] * jax.nn.sigmoid(zg_ref[:, ln])
                for s in range(nseg):
                    upad[s * stride + HALO:s * stride + HALO + seg_len, :] = u[s * seg_len:(s + 1) * seg_len]
                bias = b_ref[:, ln]
                for s in range(nseg):
                    for r0 in range(0, seg_len, CONV_ROWS):
                        base = s * stride + HALO + r0 - pad
                        acc = jnp.broadcast_to(bias, (CONV_ROWS, LANES))
                        for k in range(k_taps):
                            acc = acc + w_ref[pl.ds(k, 1), ln] * upad[base + k:base + k + CONV_ROWS, :]
                        vbuf[s * seg_len + r0:s * seg_len + r0 + CONV_ROWS, :] = acc

            def norm(c, vbuf):
                ln = lanes(c)
                gg = gg_ref[:, ln]
                gb = gb_ref[:, ln]
                for r0 in range(0, tl, CONV_ROWS):
                    rows = pl.ds(r0, CONV_ROWS)
                    acc = vbuf[rows, :]
                    mu = jnp.mean(acc, axis=-1, keepdims=True)
                    dv = acc - mu
                    var = jnp.mean(dv * dv, axis=-1, keepdims=True)
                    y = dv * lax.rsqrt(var + EPS) * gg + gb
                    y = y * jax.nn.sigmoid(y)
                    ybuf[rows, ln] = y
                    ssbuf[rows, :] = ssbuf[rows, :] + jnp.sum(y * y, axis=-1, keepdims=True)

            conv(0, vb0)

            def pair(m, carry):
                conv(2 * m + 1, vb1)
                norm(2 * m, vb0)
                conv(2 * m + 2, vb0)
                norm(2 * m + 1, vb1)
                return carry

            lax.fori_loop(0, nchunk // 2 - 1, pair, 0)
            conv(nchunk - 1, vb1)
            norm(nchunk - 2, vb0)
            norm(nchunk - 1, vb1)

        @pl.when(i < nct)
        def _():
            path(ctx_len)

        @pl.when(i >= nct)
        def _():
            path(GRID_W)

        inv = lax.rsqrt(ssbuf[:, 0:1] * (1.0 / cw) + EPS)
        o_ref[...] = (ybuf[...] * inv * mg_ref[...]).astype(o_ref.dtype)

    vec = pl.BlockSpec((1, cw), lambda i: (0, 0))
    return pl.pallas_call(
        body,
        grid=(t // tl,),
        in_specs=[pl.BlockSpec((tl, cw), lambda i: (i, 0)),
                  pl.BlockSpec((tl, cw), lambda i: (i, 1)),
                  pl.BlockSpec((k_taps, cw), lambda i: (0, 0)),
                  vec, vec, vec, vec],
        out_specs=pl.BlockSpec((tl, cw), lambda i: (i, 0)),
        out_shape=jax.ShapeDtypeStruct((t, cw), BF16),
        scratch_shapes=[pltpu.VMEM((upad_rows, LANES), F32),
                        pltpu.VMEM((tl, cw), F32),
                        pltpu.VMEM((tl, LANES), F32),
                        pltpu.VMEM((tl, LANES), F32),
                        pltpu.VMEM((tl, LANES), F32)],
        compiler_params=_cp(("parallel",), 40),
        name="conv_module",
    )(zl, zl, conv_w, conv_b.reshape(1, cw), gn_g.reshape(1, cw), gn_b.reshape(1, cw), mix_g.reshape(1, cw))


def _lru_gates(zl, cw, cb, wa, ba, wx, bx, lam, dims):
    t = zl.shape[0]
    ndir, kc, lw = cw.shape
    heads, hd = wa.shape[1], wa.shape[2]
    tl, nct, lt = dims["TL"], dims["NCT"], dims["LT"]
    assert hd == LANES and heads * hd == lw and ndir == 2 and kc - 1 <= SUBLANES
    nrb = t // SUBLANES
    tls = tl // SUBLANES

    def body(x_ref, p_ref, n_ref, cw_ref, cb_ref, wa_ref, ba_ref, wx_ref, bx_ref, lam_ref,
             af_ref, bf_ref, ab_ref, bb_ref, xp):
        i = pl.program_id(0)
        is_ctx = i < nct
        j = jnp.where(is_ctx, 0, (i - nct) % lt)
        first = jnp.logical_or(is_ctx, j == 0)
        last = jnp.logical_or(is_ctx, j == lt - 1)
        outs = ((af_ref, bf_ref), (ab_ref, bb_ref))

        def head(h, carry):
            ln = pl.ds(pl.multiple_of(h * LANES, LANES), LANES)
            xp[0:SUBLANES, :] = jnp.where(first, 0.0, p_ref[:, ln])
            xp[SUBLANES:SUBLANES + tl, :] = x_ref[:, ln]
            xp[SUBLANES + tl:2 * SUBLANES + tl, :] = jnp.where(last, 0.0, n_ref[:, ln])
            win = {o: xp[SUBLANES + o:SUBLANES + o + tl, :] for o in range(-(kc - 1), kc)}
            for d in range(ndir):
                xc = jnp.broadcast_to(cb_ref[pl.ds(d, 1), ln], (tl, LANES))
                for k in range(kc):
                    off = k - (kc - 1) if d == 0 else (kc - 1) - k
                    xc = xc + cw_ref[d, pl.ds(k, 1), ln] * win[off]
                xcb = xc.astype(BF16)
                r = _sigmoid(_dot(xcb, wa_ref[d, h].astype(BF16)) + ba_ref[pl.ds(d, 1), ln])
                ig = _sigmoid(_dot(xcb, wx_ref[d, h].astype(BF16)) + bx_ref[pl.ds(d, 1), ln])
                sp = jax.nn.softplus(-lam_ref[pl.ds(d, 1), ln])
                log_a = (-LRU_C * sp) * r
                a_ref, b_ref = outs[d]
                a = jnp.exp(log_a)
                a_ref[:, ln] = a
                b_ref[:, ln] = jnp.sqrt(1.0 - a * a) * (ig * xc)
            return carry

        lax.fori_loop(0, heads, head, 0)

    xcol = 2
    full2 = pl.BlockSpec((ndir, lw), lambda i: (0, 0))
    full4 = pl.BlockSpec((ndir, heads, hd, hd), lambda i: (0, 0, 0, 0))
    out = pl.BlockSpec((tl, lw), lambda i: (i, 0))
    shp = jax.ShapeDtypeStruct((t, lw), F32)
    return pl.pallas_call(
        body,
        grid=(t // tl,),
        in_specs=[pl.BlockSpec((tl, lw), lambda i: (i, xcol)),
                  pl.BlockSpec((SUBLANES, lw), lambda i: (jnp.maximum(i * tls - 1, 0), xcol)),
                  pl.BlockSpec((SUBLANES, lw), lambda i: (jnp.minimum((i + 1) * tls, nrb - 1), xcol)),
                  pl.BlockSpec((ndir, kc, lw), lambda i: (0, 0, 0)),
                  full2, full4, full2, full4, full2, full2],
        out_specs=[out, out, out, out],
        out_shape=[shp, shp, shp, shp],
        scratch_shapes=[pltpu.VMEM((tl + 2 * SUBLANES, LANES), F32)],
        compiler_params=_cp(("parallel",), 40),
        name="lru_gates",
    )(zl, zl, zl, cw, cb, wa, ba, wx, bx, lam)


def _scan(af, bf, ab, bb, dims):
    t, lw = af.shape
    hs = lw // LANES
    tl, nct, lt, nb = dims["TL"], dims["NCT"], dims["LT"], dims["B"]
    assert nct == nb

    def v(z):
        return z.reshape(t, hs, LANES)

    def fidx(b, s):
        return jnp.where(s == 0, b, nct + b * lt + s - 1)

    def bidx(b, s):
        return jnp.where(s == 0, b, nct + b * lt + lt - s)

    def body(af_ref, bf_ref, ab_ref, bb_ref, hf_ref, hb_ref, hc):
        @pl.when(pl.program_id(1) == 0)
        def _():
            hc[...] = jnp.zeros_like(hc)

        def block(q, carry):
            hf, hb = carry
            tf = pl.multiple_of(q * SUBLANES, SUBLANES)
            tb = pl.multiple_of(tl - SUBLANES - q * SUBLANES, SUBLANES)
            a_f, b_f = af_ref[pl.ds(tf, SUBLANES)], bf_ref[pl.ds(tf, SUBLANES)]
            a_b, b_b = ab_ref[pl.ds(tb, SUBLANES)], bb_ref[pl.ds(tb, SUBLANES)]
            of, ob = [], []
            for k in range(SUBLANES):
                hf = a_f[k] * hf + b_f[k]
                of.append(hf)
                kb = SUBLANES - 1 - k
                hb = a_b[kb] * hb + b_b[kb]
                ob.append(hb)
            hf_ref[pl.ds(tf, SUBLANES)] = jnp.stack(of)
            hb_ref[pl.ds(tb, SUBLANES)] = jnp.stack(ob[::-1])
            return hf, hb

        hf, hb = lax.fori_loop(0, tl // SUBLANES, block, (hc[0], hc[1]))
        hc[0] = hf
        hc[1] = hb

    fspec = pl.BlockSpec((tl, hs, LANES), lambda b, s: (fidx(b, s), 0, 0))
    bspec = pl.BlockSpec((tl, hs, LANES), lambda b, s: (bidx(b, s), 0, 0))
    shp = jax.ShapeDtypeStruct((t, hs, LANES), F32)
    hf, hb = pl.pallas_call(
        body,
        grid=(nb, 1 + lt),
        in_specs=[fspec, fspec, bspec, bspec],
        out_specs=[fspec, bspec],
        out_shape=[shp, shp],
        scratch_shapes=[pltpu.VMEM((2, hs, LANES), F32)],
        compiler_params=_cp(("parallel", "arbitrary"), 40),
        name="lru_scan",
    )(v(af), v(bf), v(ab), v(bb))
    return hf.reshape(t, lw), hb.reshape(t, lw)


def _lru_out(zl, hf, hb, mix_g, dims):
    t, lw = hf.shape
    tl = dims["TL"]
    gcol = 3

    def body(g_ref, hf_ref, hb_ref, mg_ref, o_ref):
        y = jax.nn.gelu(g_ref[...], approximate=True) * (hf_ref[...] + hb_ref[...])
        inv = lax.rsqrt(jnp.mean(y * y, axis=-1, keepdims=True) + EPS)
        o_ref[...] = (y * inv * mg_ref[...]).astype(o_ref.dtype)

    row = pl.BlockSpec((tl, lw), lambda i: (i, 0))
    return pl.pallas_call(
        body,
        grid=(t // tl,),
        in_specs=[pl.BlockSpec((tl, lw), lambda i: (i, gcol)), row, row,
                  pl.BlockSpec((1, lw), lambda i: (0, 0))],
        out_specs=row,
        out_shape=jax.ShapeDtypeStruct((t, lw), BF16),
        compiler_params=_cp(("parallel",), 40),
        name="lru_out",
    )(zl, hf, hb, mix_g.reshape(1, lw))


def _lru_pass(zl, cw, cb, wa, ba, wx, bx, lam, direction, dims, hf=None, mix_g=None):
    t = zl.shape[0]
    kc, lw = cw.shape
    heads, hd = wa.shape[0], wa.shape[1]
    tl, nct, lt, nb = dims["TL"], dims["NCT"], dims["LT"], dims["B"]
    assert hd == LANES and heads * hd == lw and kc - 1 <= SUBLANES and nct == nb
    nrb = t // SUBLANES
    tls = tl // SUBLANES
    _, pitch = _token_rows(lw)
    back = direction == 1
    xcol, gcol = 2, 3

    def tile(b, s):
        j = lt - s if back else s - 1
        return jnp.where(s == 0, b, nct + b * lt + j)

    def body(*refs):
        if back:
            (x_ref, p_ref, n_ref, cw_ref, cb_ref, wa_ref, ba_ref, wx_ref, bx_ref, lam_ref, g_ref, hf_ref, mg_ref,
             o_ref, xp, a_s, b_s, h_s, hc, ybuf, ssbuf) = refs
        else:
            (x_ref, p_ref, n_ref, cw_ref, cb_ref, wa_ref, ba_ref, wx_ref, bx_ref, lam_ref,
             o_ref, xp, a_s, b_s, h_s, hc) = refs
        s = pl.program_id(1)
        is_ctx = s == 0
        j = lt - s if back else s - 1
        first = jnp.logical_or(is_ctx, j == 0)
        last = jnp.logical_or(is_ctx, j == lt - 1)

        @pl.when(s == 0)
        def _():
            hc[...] = jnp.zeros_like(hc)

        def gates(h, carry):
            ln = pl.ds(pl.multiple_of(h * LANES, LANES), LANES)
            xp[0:SUBLANES, :] = jnp.where(first, 0.0, p_ref[:, ln])
            xp[SUBLANES:SUBLANES + tl, :] = x_ref[:, ln]
            xp[SUBLANES + tl:2 * SUBLANES + tl, :] = jnp.where(last, 0.0, n_ref[:, ln])
            xc = jnp.broadcast_to(cb_ref[:, ln], (tl, LANES))
            for k in range(kc):
                off = (kc - 1) - k if back else k - (kc - 1)
                xc = xc + cw_ref[pl.ds(k, 1), ln] * xp[SUBLANES + off:SUBLANES + off + tl, :]
            xcb = xc.astype(BF16)
            r = _sigmoid(_dot(xcb, wa_ref[h].astype(BF16)) + ba_ref[:, ln])
            ig = _sigmoid(_dot(xcb, wx_ref[h].astype(BF16)) + bx_ref[:, ln])
            log_a = (-LRU_C * jax.nn.softplus(-lam_ref[:, ln])) * r
            a = jnp.exp(log_a)
            a_s[pl.ds(h, tl, stride=pitch), :] = a
            b_s[pl.ds(h, tl, stride=pitch), :] = jnp.sqrt(1.0 - a * a) * (ig * xc)
            return carry

        lax.fori_loop(0, heads, gates, 0)

        def scan(q, hcur):
            for k in range(SUBLANES):
                step = q * SUBLANES + k
                row = ((tl - 1 - step) if back else step) * pitch
                hcur = a_s[pl.ds(row, heads), :] * hcur + b_s[pl.ds(row, heads), :]
                h_s[pl.ds(row, heads), :] = hcur
            return hcur

        hc[...] = lax.fori_loop(0, tl // SUBLANES, scan, hc[...])

        if back:
            ssbuf[...] = jnp.zeros_like(ssbuf)

        def finish(h, carry):
            ln = pl.ds(pl.multiple_of(h * LANES, LANES), LANES)
            hh = h_s[pl.ds(h, tl, stride=pitch), :]
            if back:
                y = jax.nn.gelu(g_ref[:, ln], approximate=True) * (hf_ref[:, ln] + hh)
                ybuf[:, ln] = y
                ssbuf[...] = ssbuf[...] + jnp.sum(y * y, axis=-1, keepdims=True)
            else:
                o_ref[:, ln] = hh
            return carry

        lax.fori_loop(0, heads, finish, 0)
        if back:
            inv = lax.rsqrt(ssbuf[:, 0:1] * (1.0 / lw) + EPS)
            o_ref[...] = (ybuf[...] * inv * mg_ref[...]).astype(o_ref.dtype)

    def col(c):
        return pl.BlockSpec((tl, lw), lambda b, s: (tile(b, s), c))

    vec = pl.BlockSpec((1, lw), lambda b, s: (0, 0))
    mat = pl.BlockSpec((heads, hd, hd), lambda b, s: (0, 0, 0))
    in_specs = [col(xcol),
                pl.BlockSpec((SUBLANES, lw), lambda b, s: (jnp.maximum(tile(b, s) * tls - 1, 0), xcol)),
                pl.BlockSpec((SUBLANES, lw), lambda b, s: (jnp.minimum((tile(b, s) + 1) * tls, nrb - 1), xcol)),
                pl.BlockSpec((kc, lw), lambda b, s: (0, 0)), vec, mat, vec, mat, vec, vec]
    args = [zl, zl, zl, cw, cb.reshape(1, lw), wa, ba.reshape(1, lw), wx, bx.reshape(1, lw), lam.reshape(1, lw)]
    scratch = [pltpu.VMEM((tl + 2 * SUBLANES, LANES), F32),
               pltpu.VMEM((tl * pitch, LANES), F32), pltpu.VMEM((tl * pitch, LANES), F32),
               pltpu.VMEM((tl * pitch, LANES), F32), pltpu.VMEM((heads, LANES), F32)]
    if back:
        in_specs += [col(gcol), col(0), vec]
        args += [zl, hf, mix_g.reshape(1, lw)]
        scratch += [pltpu.VMEM((tl, lw), F32), pltpu.VMEM((tl, LANES), F32)]
    return pl.pallas_call(
        body,
        grid=(nb, 1 + lt),
        in_specs=in_specs,
        out_specs=col(0),
        out_shape=jax.ShapeDtypeStruct((t, lw), BF16 if back else F32),
        scratch_shapes=scratch,
        compiler_params=_cp(("parallel", "arbitrary"), 48),
        name="lru_bwd" if back else "lru_fwd",
    )(*args)


def _wout_residual(mix_c, mix_r, w_out, xs, mods, layer, dims):
    t, d = xs.shape
    kc = mix_c.shape[1]
    kr = mix_r.shape[1]
    assert kc == kr
    tm, tn = dims["TM"], min(d, 1024)
    nct, lt = dims["B"] * dims["CTX"] // tm, dims["L"] // tm

    def body(a1_ref, a2_ref, w1_ref, w2_ref, x_ref, g_ref, o_ref):
        acc = _dot(a1_ref[...], w1_ref[...]) + _dot(a2_ref[...], w2_ref[...])
        o_ref[...] = x_ref[...] + g_ref[0] * acc

    return pl.pallas_call(
        body,
        grid=(d // tn, t // tm),
        in_specs=[pl.BlockSpec((tm, kc), lambda j, i: (i, 0)),
                  pl.BlockSpec((tm, kr), lambda j, i: (i, 0)),
                  pl.BlockSpec((None, kc, tn), lambda j, i: (layer, 0, j)),
                  pl.BlockSpec((None, kr, tn), lambda j, i: (layer, 1, j)),
                  pl.BlockSpec((tm, tn), lambda j, i: (i, j)),
                  pl.BlockSpec((1, 1, tn), lambda j, i: (_mod_idx(layer, _seg(i, nct, lt), 2), 0, j))],
        out_specs=pl.BlockSpec((tm, tn), lambda j, i: (i, j)),
        out_shape=jax.ShapeDtypeStruct((t, d), F32),
        compiler_params=_cp(("parallel", "parallel"), 48),
        name="wout_residual",
    )(mix_c, mix_r, w_out, w_out, xs, mods)


def _router(xs, g, mods, layer, rw, rb, dims):
    t, d = xs.shape
    half = d // 2
    rows_h, pitch = _token_rows(half)
    tl, nct, lt = dims["TL"], dims["NCT"], dims["LT"]

    def body(x_ref, g_ref, sh_ref, sc_ref, rw_ref, rb_ref, hp_ref, e4_ref, p4_ref, mask_ref, cnt_ref):
        x = x_ref[...]
        ms = jnp.mean(x * x, axis=-1, keepdims=True)
        h = (x * lax.rsqrt(ms + EPS) * g_ref[...]) * (1.0 + sc_ref[0]) + sh_ref[0]
        hh = h.astype(BF16)
        hh32 = hh.astype(F32)
        words = _pack_words(h)
        for cidx in range(pitch):
            chunk = words[:, cidx * LANES:(cidx + 1) * LANES] if cidx < rows_h else jnp.zeros((tl, LANES), U32)
            hp_ref[pl.ds(cidx, tl, stride=pitch), :] = chunk

        w = rw_ref[...]
        wh = w.astype(BF16)
        wl = (w - wh.astype(F32)).astype(BF16)
        hl = (h - hh32).astype(BF16)
        logits = _dot(hh, wh) + (_dot(hh, wl) + _dot(hl, wh)) + rb_ref[...]

        lane = lax.broadcasted_iota(I32, (tl, LANES), 1)
        vals = logits
        tops, hots = [], []
        e4 = jnp.zeros((tl, LANES), I32)
        for k in range(TOP_K):
            m = jnp.max(vals, axis=-1, keepdims=True)
            sel = jnp.min(jnp.where(vals == m, lane, LANES), axis=-1, keepdims=True)
            hot = lane == sel
            tops.append(m)
            hots.append(hot)
            vals = jnp.where(hot, -jnp.inf, vals)
            e4 = jnp.where(lane == k, sel, e4)
        ex = [jnp.exp(m - tops[0]) for m in tops]
        den = ex[0]
        for e in ex[1:]:
            den = den + e
        p4 = jnp.zeros((tl, LANES), F32)
        mask = jnp.zeros((tl, LANES), F32)
        for k in range(TOP_K):
            p4 = jnp.where(lane == k, ex[k] / den, p4)
            mask = jnp.where(hots[k], 1.0, mask)
        e4_ref[...] = e4
        p4_ref[...] = p4
        mask_ref[...] = mask

        @pl.when(pl.program_id(0) == 0)
        def _():
            cnt_ref[...] = jnp.zeros_like(cnt_ref)

        cnt_ref[...] = cnt_ref[...] + jnp.sum(mask, axis=0, keepdims=True)

    lanes_out = pl.BlockSpec((tl, LANES), lambda i: (i, 0))
    return pl.pallas_call(
        body,
        grid=(t // tl,),
        in_specs=[pl.BlockSpec((tl, d), lambda i: (i, 0)),
                  pl.BlockSpec((1, d), lambda i: (0, 0)),
                  pl.BlockSpec((1, 1, d), lambda i: (_mod_idx(layer, _seg(i, nct, lt), 3), 0, 0)),
                  pl.BlockSpec((1, 1, d), lambda i: (_mod_idx(layer, _seg(i, nct, lt), 4), 0, 0)),
                  pl.BlockSpec((d, LANES), lambda i: (0, 0)),
                  pl.BlockSpec((1, LANES), lambda i: (0, 0))],
        out_specs=[pl.BlockSpec((tl * pitch, LANES), lambda i: (i, 0)), lanes_out, lanes_out, lanes_out,
                   pl.BlockSpec((SUBLANES, LANES), lambda i: (0, 0))],
        out_shape=[jax.ShapeDtypeStruct((t * pitch, LANES), U32),
                   jax.ShapeDtypeStruct((t, LANES), I32),
                   jax.ShapeDtypeStruct((t, LANES), F32),
                   jax.ShapeDtypeStruct((t, LANES), F32),
                   jax.ShapeDtypeStruct((SUBLANES, LANES), F32)],
        compiler_params=_cp(("arbitrary",), 40),
        name="router",
    )(xs, g.reshape(1, d), mods, mods, rw, rb)


def _positions(mask, e4, cnt, n_experts, n_tiles, dims):
    t = mask.shape[0]
    tl = dims["TL"]
    te = EXPERT_TILE
    ntp = -(-n_tiles // SUBLANES) * SUBLANES

    def lane_cumsum(x, lane):
        sh = 1
        while sh < LANES:
            x = x + jnp.where(lane >= sh, pltpu.roll(x, sh, axis=1), 0.0)
            sh *= 2
        return x

    def body(mask_ref, e4_ref, cnt_ref, pos_ref, tab_ref, carry):
        i = pl.program_id(0)
        lane8 = lax.broadcasted_iota(I32, (SUBLANES, LANES), 1)
        cnt = cnt_ref[...]
        gp = jnp.ceil(cnt * (1.0 / te)) * te
        inc = lane_cumsum(gp, lane8)
        off = inc - gp

        @pl.when(i == 0)
        def _():
            carry[...] = jnp.zeros_like(carry)
            lane = lax.broadcasted_iota(I32, (ntp, LANES), 1)
            tile = lax.broadcasted_iota(I32, (ntp, LANES), 0).astype(F32)
            total = jnp.max(inc[0:1], axis=-1, keepdims=True)
            start = tile * te
            used = start < total
            start_c = jnp.minimum(start, total - te)
            ex = jnp.sum(jnp.where(inc[0:1] <= start_c, 1.0, 0.0), axis=-1, keepdims=True)
            ex = jnp.minimum(ex, n_experts - 1.0)
            hot = lane.astype(F32) == ex
            cnt_e = jnp.sum(jnp.where(hot, cnt[0:1], 0.0), axis=-1, keepdims=True)
            off_e = jnp.sum(jnp.where(hot, off[0:1], 0.0), axis=-1, keepdims=True)
            valid = jnp.where(used, jnp.clip(cnt_e - (start - off_e), 0.0, te), 0.0)
            tab = jnp.where(lane == 0, ex, jnp.where(lane == 1, valid, 0.0))
            tab_ref[...] = tab.astype(I32)

        m = mask_ref[...]
        r = lax.broadcasted_iota(I32, (tl, tl), 0)
        c = lax.broadcasted_iota(I32, (tl, tl), 1)
        tri = jnp.where(r > c, 1.0, 0.0).astype(BF16)
        rank = _dot(tri, m.astype(BF16))
        posf = off[0:1] + carry[0:1] + rank
        lane = lax.broadcasted_iota(I32, (tl, LANES), 1)
        e4 = e4_ref[...]
        pos4 = jnp.zeros((tl, LANES), F32)
        for k in range(TOP_K):
            ek = jnp.sum(jnp.where(lane == k, e4, 0), axis=-1, keepdims=True)
            pk = jnp.sum(jnp.where(lane == ek, posf, 0.0), axis=-1, keepdims=True)
            pos4 = jnp.where(lane == k, pk, pos4)
        pos_ref[...] = pos4.astype(I32)
        carry[...] = carry[...] + jnp.sum(m, axis=0, keepdims=True)

    rows = pl.BlockSpec((tl, LANES), lambda i: (i, 0))
    return pl.pallas_call(
        body,
        grid=(t // tl,),
        in_specs=[rows, rows, pl.BlockSpec((SUBLANES, LANES), lambda i: (0, 0))],
        out_specs=[rows, pl.BlockSpec((ntp, LANES), lambda i: (0, 0))],
        out_shape=[jax.ShapeDtypeStruct((t, LANES), I32), jax.ShapeDtypeStruct((ntp, LANES), I32)],
        scratch_shapes=[pltpu.VMEM((SUBLANES, LANES), F32)],
        compiler_params=_cp(("arbitrary",), 32),
        name="moe_positions",
    )(mask, e4, cnt)


def _invert(pos_flat, n_slots):
    n = pos_flat.shape[0]

    def body(pos_ref, zeros_ref, inv_ref, sem):
        fill = pltpu.make_async_copy(zeros_ref, inv_ref, sem)
        fill.start()
        fill.wait()

        def put(q, c):
            inv_ref[pos_ref[q]] = lax.shift_right_logical(q, 2)
            return c

        lax.fori_loop(0, n, put, 0, unroll=8)

    assert TOP_K == 4
    return pl.pallas_call(
        body,
        in_specs=[pl.BlockSpec(memory_space=pltpu.SMEM), pl.BlockSpec(memory_space=pl.ANY)],
        out_specs=pl.BlockSpec(memory_space=pltpu.SMEM),
        out_shape=jax.ShapeDtypeStruct((n_slots,), I32),
        scratch_shapes=[pltpu.SemaphoreType.DMA(())],
        name="moe_invert",
    )(pos_flat, jnp.zeros((n_slots,), I32))


def _prep_wgu(w, layer):
    _, n_exp, d, n = w.shape
    tk = min(d, 2048)
    blk = min(2 * LANES, n)

    def body(w_ref, o_ref):
        r = lax.broadcasted_iota(I32, (blk, blk), 0)
        c = lax.broadcasted_iota(I32, (blk, blk), 1)
        src = jnp.where(c < blk // 2, 2 * c, 2 * (c - blk // 2) + 1)
        perm = jnp.where(r == src, 1.0, 0.0).astype(BF16)
        for b in range(n // blk):
            cols = slice(b * blk, (b + 1) * blk)
            o_ref[0, :, cols] = _dot(w_ref[0, :, cols].astype(BF16), perm).astype(BF16)

    return pl.pallas_call(
        body,
        grid=(n_exp, d // tk),
        in_specs=[pl.BlockSpec((None, 1, tk, n), lambda e, k: (layer, e, k, 0))],
        out_specs=pl.BlockSpec((1, tk, n), lambda e, k: (e, k, 0)),
        out_shape=jax.ShapeDtypeStruct((n_exp, d, n), BF16),
        compiler_params=_cp(("parallel", "parallel"), 40),
        name="prep_wgu",
    )(w)


def _deinterleave_bias(b):
    n_exp, n = b.shape
    blk = min(2 * LANES, n)
    return b.reshape(n_exp, n // blk, blk // 2, 2).transpose(0, 1, 3, 2).reshape(n_exp, n)


def _experts(hp, inv, tab_e, tab_v, wgu, bgu, wdn, bdn, layer, n_tiles):
    n_exp, d, de2 = wgu.shape
    half = d // 2
    de = de2 // 2
    te = EXPERT_TILE
    rows_h, pitch_h = _token_rows(half)
    blk = min(2 * LANES, de2)
    inv3 = inv.reshape(n_tiles, 1, te)

    def body(te_ref, tv_ref, cur_ref, nxt_ref, hp_ref, wgu_ref, bgu_ref, wdn_ref, bdn_ref,
             ys_ref, xbuf, wdn_b, sem):
        i = pl.program_id(0)
        slot = i % 2

        @pl.when(jnp.logical_or(i == 0, te_ref[i] != te_ref[jnp.maximum(i - 1, 0)]))
        def _():
            wdn_b[...] = wdn_ref[0, 0].astype(BF16)

        def row_copy(tok, sl, r):
            return pltpu.make_async_copy(hp_ref.at[pl.ds(tok * pitch_h, rows_h)],
                                         xbuf.at[sl, pl.ds(r * pitch_h, rows_h)], sem.at[sl])

        def issue(idx_ref, sl):
            def f(r, c):
                row_copy(idx_ref[0, 0, r], sl, r).start()
                return c

            lax.fori_loop(0, te, f, 0, unroll=8)

        def wait_rows(sl):
            def w(r, c):
                row_copy(0, sl, 0).wait()
                return c

            lax.fori_loop(0, te, w, 0, unroll=8)

        @pl.when(jnp.logical_and(i == 0, tv_ref[0] > 0))
        def _():
            issue(cur_ref, 0)

        nxt = jnp.minimum(i + 1, n_tiles - 1)

        @pl.when(jnp.logical_and(i + 1 < n_tiles, tv_ref[nxt] > 0))
        def _():
            issue(nxt_ref, 1 - slot)

        @pl.when(tv_ref[i] > 0)
        def _():
            wait_rows(slot)
            los, his = [], []
            for cidx in range(rows_h):
                lo_c, hi_c = _unpack_words(xbuf[slot, pl.ds(cidx, te, stride=pitch_h), :])
                los.append(lo_c.astype(BF16))
                his.append(hi_c.astype(BF16))
            gu = _dot(jnp.concatenate(los + his, axis=1), wgu_ref[0]) + bgu_ref[0]
            acts = []
            for b in range(de2 // blk):
                gate = jnp.minimum(gu[:, b * blk:b * blk + blk // 2], SWIGLU_LIMIT)
                up = jnp.clip(gu[:, b * blk + blk // 2:(b + 1) * blk], -SWIGLU_LIMIT, SWIGLU_LIMIT)
                acts.append(((up + 1.0) * gate * jax.nn.sigmoid(SWIGLU_ALPHA * gate)).astype(BF16))
            out = _pack_words(_dot(jnp.concatenate(acts, axis=1), wdn_b[...]) + bdn_ref[0])
            for cidx in range(pitch_h):
                chunk = out[:, cidx * LANES:(cidx + 1) * LANES] if cidx < rows_h else jnp.zeros((te, LANES), U32)
                ys_ref[pl.ds(cidx, te, stride=pitch_h), :] = chunk

        @pl.when(tv_ref[i] == 0)
        def _():
            ys_ref[...] = jnp.zeros_like(ys_ref)

    grid_spec = pltpu.PrefetchScalarGridSpec(
        num_scalar_prefetch=2,
        grid=(n_tiles,),
        in_specs=[pl.BlockSpec((1, 1, te), lambda i, e, v: (i, 0, 0), memory_space=pltpu.SMEM),
                  pl.BlockSpec((1, 1, te), lambda i, e, v: (jnp.minimum(i + 1, n_tiles - 1), 0, 0),
                               memory_space=pltpu.SMEM),
                  pl.BlockSpec(memory_space=pl.ANY),
                  pl.BlockSpec((1, d, de2), lambda i, e, v: (e[i], 0, 0)),
                  pl.BlockSpec((1, 1, de2), lambda i, e, v: (e[i], 0, 0)),
                  pl.BlockSpec((1, 1, de, d), lambda i, e, v: (layer, e[i], 0, 0)),
                  pl.BlockSpec((1, 1, d), lambda i, e, v: (e[i], 0, 0))],
        out_specs=pl.BlockSpec((te * pitch_h, LANES), lambda i, e, v: (i, 0)),
        scratch_shapes=[pltpu.VMEM((2, te * pitch_h, LANES), U32), pltpu.VMEM((de, d), BF16),
                        pltpu.SemaphoreType.DMA((2,))],
    )
    return pl.pallas_call(
        body,
        grid_spec=grid_spec,
        out_shape=jax.ShapeDtypeStruct((n_tiles * te * pitch_h, LANES), U32),
        compiler_params=_cp(("arbitrary",), 56),
        name="moe_experts",
    )(tab_e, tab_v, inv3, inv3, hp, wgu, bgu.reshape(n_exp, 1, de2), wdn, bdn.reshape(n_exp, 1, d))


def _combine(xs, ys, pos_flat, p4, mods, layer, dims, final_g, next_g):
    t, d = xs.shape
    tc = COMBINE_TILE
    nq = tc * TOP_K
    nct_rows = dims["B"] * dims["CTX"]
    nct, lt = nct_rows // tc, dims["L"] // tc
    final = final_g is not None
    skip = nct if final else 0
    n_steps = t // tc - skip
    pos3 = pos_flat.reshape(t // tc, 1, nq)
    rows_h, pitch_h = _token_rows(d // 2)

    def body(cur_ref, nxt_ref, x_ref, p_ref, g_ref, fg_ref, sh_ref, sc_ref, ys_ref, o_ref, *rest):
        h_ref = None if final else rest[0]
        gbuf, sem = rest[-2:]
        i = pl.program_id(0)
        slot = i % 2

        def row_copy(src_row, sl, dst_row):
            return pltpu.make_async_copy(ys_ref.at[pl.ds(src_row, rows_h)],
                                         gbuf.at[sl, pl.ds(dst_row, rows_h)], sem.at[sl])

        def issue(idx_ref, sl):
            def f(tok, c):
                for k in range(TOP_K):
                    row_copy(idx_ref[0, 0, tok * TOP_K + k] * pitch_h, sl, (k * tc + tok) * pitch_h).start()
                return c

            lax.fori_loop(0, tc, f, 0, unroll=2)

        @pl.when(i == 0)
        def _():
            issue(cur_ref, 0)

        @pl.when(i + 1 < n_steps)
        def _():
            issue(nxt_ref, 1 - slot)

        def w(q, c):
            row_copy(0, slot, 0).wait()
            return c

        lax.fori_loop(0, nq, w, 0, unroll=8)
        p = p_ref[...]
        pk = [p[:, k:k + 1] for k in range(TOP_K)]
        lo_cols, hi_cols = [], []
        for cidx in range(rows_h):
            y_lo = y_hi = None
            for k in range(TOP_K):
                lo, hi = _unpack_words(gbuf[slot, pl.ds(k * tc * pitch_h + cidx, tc, stride=pitch_h), :])
                y_lo = pk[k] * lo if k == 0 else y_lo + pk[k] * lo
                y_hi = pk[k] * hi if k == 0 else y_hi + pk[k] * hi
            lo_cols.append(y_lo)
            hi_cols.append(y_hi)
        xn = x_ref[...] + g_ref[0] * jnp.concatenate(lo_cols + hi_cols, axis=1)
        unit = xn * lax.rsqrt(jnp.mean(xn * xn, axis=-1, keepdims=True) + EPS)
        if final:
            o_ref[...] = unit * fg_ref[...]
        else:
            o_ref[...] = xn
            h_ref[...] = ((unit * fg_ref[...]) * (1.0 + sc_ref[0]) + sh_ref[0]).astype(h_ref.dtype)

    fg = (final_g if final else next_g).reshape(1, d)
    nl = layer if final else layer + 1
    row = pl.BlockSpec((tc, d), lambda i: (i, 0))
    shp = jax.ShapeDtypeStruct((n_steps * tc, d), F32)
    out = pl.pallas_call(
        body,
        grid=(n_steps,),
        in_specs=[pl.BlockSpec((1, 1, nq), lambda i: (i + skip, 0, 0), memory_space=pltpu.SMEM),
                  pl.BlockSpec((1, 1, nq), lambda i: (jnp.minimum(i + 1, n_steps - 1) + skip, 0, 0),
                               memory_space=pltpu.SMEM),
                  pl.BlockSpec((tc, d), lambda i: (i + skip, 0)),
                  pl.BlockSpec((tc, LANES), lambda i: (i + skip, 0)),
                  pl.BlockSpec((1, 1, d), lambda i: (_mod_idx(layer, _seg(i + skip, nct, lt), 5), 0, 0)),
                  pl.BlockSpec((1, d), lambda i: (0, 0)),
                  pl.BlockSpec((1, 1, d), lambda i: (_mod_idx(nl, _seg(i + skip, nct, lt), 0), 0, 0)),
                  pl.BlockSpec((1, 1, d), lambda i: (_mod_idx(nl, _seg(i + skip, nct, lt), 1), 0, 0)),
                  pl.BlockSpec(memory_space=pl.ANY)],
        out_specs=row if final else [row, row],
        out_shape=shp if final else [shp, jax.ShapeDtypeStruct((n_steps * tc, d), BF16)],
        scratch_shapes=[pltpu.VMEM((2, TOP_K * tc * pitch_h, LANES), U32), pltpu.SemaphoreType.DMA((2,))],
        compiler_params=_cp(("arbitrary",), 40),
        name="moe_combine",
    )(pos3, pos3, xs, p4, mods, fg, mods, mods, ys)
    return (out, None) if final else tuple(out)


def kernel(x, c, ctx, c_ctx, norm1_g, norm2_g, w_ada, b_ada, w_in, conv_w, conv_b, gn_g, gn_b, lru_conv_w, lru_conv_b, lru_w_a, lru_b_a, lru_w_x, lru_b_x, lru_lambda, mix_norm_g, w_out, router_w, router_b, exp_w_gu, exp_b_gu, exp_w_down, exp_b_down, final_norm_g):
    nb, seq, d = x.shape
    ctx_len = ctx.shape[1]
    depth = w_in.shape[0]
    cw = conv_w.shape[2]
    n_exp = router_w.shape[2]
    de = exp_w_down.shape[2]
    tl = ctx_len
    t = nb * (ctx_len + seq)
    assert nb + 1 <= MOD_ROWS and seq % tl == 0 and n_exp <= LANES
    tm = min(512, nb * ctx_len)
    assert (nb * ctx_len) % tm == 0 and seq % tm == 0
    assert (nb * ctx_len) % COMBINE_TILE == 0 and seq % COMBINE_TILE == 0
    dims = {"B": nb, "CTX": ctx_len, "L": seq, "TL": tl, "NCT": nb, "LT": seq // tl, "TM": tm}
    n_tiles = (t * TOP_K) // EXPERT_TILE + n_exp
    n_slots = n_tiles * EXPERT_TILE

    cvec = jnp.concatenate([c_ctx[None], c, jnp.zeros((MOD_ROWS - 1 - nb, d), F32)], axis=0)
    mods = _adaln(cvec, w_ada, b_ada).reshape(depth * MOD_ROWS * N_MOD, 1, d)
    xs, hl = _prenorm(ctx.reshape(nb * ctx_len, d), x.reshape(nb * seq, d), norm1_g[0], mods, 0, dims)
    w_in_b = w_in.astype(BF16)
    w_out_b = w_out.astype(BF16)

    for l in range(depth):
        last = l == depth - 1
        zl = _matmul(hl, w_in_b, l, tm, min(1024, w_in.shape[2]), "in_proj")
        mix_c = _conv_module(zl, conv_w[l], conv_b[l], gn_g[l], gn_b[l], mix_norm_g[l, :cw], dims)
        lru = [(lru_conv_w[l, k], lru_conv_b[l, k], lru_w_a[l, k], lru_b_a[l, k],
                lru_w_x[l, k], lru_b_x[l, k], lru_lambda[l, k]) for k in range(2)]
        hf = _lru_pass(zl, *lru[0], 0, dims)
        mix_r = _lru_pass(zl, *lru[1], 1, dims, hf=hf, mix_g=mix_norm_g[l, cw:])
        xs = _wout_residual(mix_c, mix_r, w_out_b, xs, mods, l, dims)

        rw = jnp.pad(router_w[l], ((0, 0), (0, LANES - n_exp)))
        rb = jnp.pad(router_b[l], (0, LANES - n_exp), constant_values=NEG_BIG).reshape(1, LANES)
        hp, e4, p4, mask, cnt = _router(xs, norm2_g[l], mods, l, rw, rb, dims)
        pos4, tab = _positions(mask, e4, cnt, n_exp, n_tiles, dims)
        pos_flat = pos4[:, :TOP_K].reshape(t * TOP_K)
        inv = _invert(pos_flat, n_slots)
        ys = _experts(hp, inv, tab[:n_tiles, 0], tab[:n_tiles, 1],
                      _prep_wgu(exp_w_gu, l), _deinterleave_bias(exp_b_gu[l]),
                      exp_w_down, exp_b_down[l], l, n_tiles)
        xs, hl = _combine(xs, ys, pos_flat, p4, mods, l, dims, final_norm_g if last else None,
                          None if last else norm1_g[l + 1])

    return xs.reshape(nb, seq, d)
```

```python
import jax
import jax.numpy as jnp
from jax import lax
from jax.experimental import pallas as pl
from jax.experimental.pallas import tpu as pltpu

GRID_W = 64
CONV_GROUPS = 16
TOP_K = 4
N_MOD = 6
EPS = 1e-6
LRU_C = 8.0
SWIGLU_LIMIT = 7.0
SWIGLU_ALPHA = 1.702

LANES = 128
SUBLANES = 8
MOD_ROWS = 8
HALO = 16
CONV_ROWS = 64
EXPERT_TILE = 256
COMBINE_TILE = 64
NEG_BIG = -1e30

F32 = jnp.float32
BF16 = jnp.bfloat16
U32 = jnp.uint32
I32 = jnp.int32


def _cp(sems, vmem_mb):
    return pltpu.CompilerParams(dimension_semantics=sems, vmem_limit_bytes=vmem_mb << 20)


def _seg(i, n_ctx_tiles, lat_tiles):
    return jnp.where(i < n_ctx_tiles, 0, 1 + (i - n_ctx_tiles) // lat_tiles)


def _mod_idx(layer, seg, j):
    return (layer * MOD_ROWS + seg) * N_MOD + j


def _dot(a, b):
    return jnp.dot(a, b, preferred_element_type=F32)


def _sigmoid(x):
    return 0.5 * jnp.tanh(0.5 * x) + 0.5


def _pack_words(x):
    half = x.shape[1] // 2
    bits = pltpu.bitcast(x.astype(BF16).astype(F32), U32)
    return (bits[:, half:] & jnp.uint32(0xFFFF0000)) | lax.shift_right_logical(bits[:, :half], jnp.uint32(16))


def _unpack_words(words):
    lo = pltpu.bitcast(lax.shift_left(words, jnp.uint32(16)), F32)
    hi = pltpu.bitcast(words & jnp.uint32(0xFFFF0000), F32)
    return lo, hi


def _token_rows(width):
    rows = width // LANES
    return rows, rows + 4


def _adaln(cvec, w_ada, b_ada):
    depth, d, n = w_ada.shape
    tn = min(n, 1024)

    def body(c_ref, w_ref, b_ref, o_ref):
        c = c_ref[...]
        s = (c * jax.nn.sigmoid(c)).astype(BF16)
        o_ref[0] = _dot(s, w_ref[0].astype(BF16)) + b_ref[0]

    return pl.pallas_call(
        body,
        grid=(depth, n // tn),
        in_specs=[pl.BlockSpec((MOD_ROWS, d), lambda l, j: (0, 0)),
                  pl.BlockSpec((1, d, tn), lambda l, j: (l, 0, j)),
                  pl.BlockSpec((1, 1, tn), lambda l, j: (l, 0, j))],
        out_specs=pl.BlockSpec((1, MOD_ROWS, tn), lambda l, j: (l, 0, j)),
        out_shape=jax.ShapeDtypeStruct((depth, MOD_ROWS, n), F32),
        compiler_params=_cp(("parallel", "parallel"), 48),
        name="adaln",
    )(cvec, w_ada, b_ada.reshape(depth, 1, n))


def _prenorm(ctx_rows, lat_rows, g, mods, layer, dims):
    d = lat_rows.shape[1]
    t = ctx_rows.shape[0] + lat_rows.shape[0]
    tl, nct, lt = dims["TL"], dims["NCT"], dims["LT"]

    def body(c_ref, l_ref, g_ref, sh_ref, sc_ref, x_ref, o_ref):
        x = jnp.where(pl.program_id(0) < nct, c_ref[...], l_ref[...])
        x_ref[...] = x
        ms = jnp.mean(x * x, axis=-1, keepdims=True)
        y = x * lax.rsqrt(ms + EPS) * g_ref[...]
        o_ref[...] = (y * (1.0 + sc_ref[0]) + sh_ref[0]).astype(o_ref.dtype)

    row = pl.BlockSpec((tl, d), lambda i: (i, 0))
    return pl.pallas_call(
        body,
        grid=(t // tl,),
        in_specs=[pl.BlockSpec((tl, d), lambda i: (jnp.minimum(i, nct - 1), 0)),
                  pl.BlockSpec((tl, d), lambda i: (jnp.maximum(i - nct, 0), 0)),
                  pl.BlockSpec((1, d), lambda i: (0, 0)),
                  pl.BlockSpec((1, 1, d), lambda i: (_mod_idx(layer, _seg(i, nct, lt), 0), 0, 0)),
                  pl.BlockSpec((1, 1, d), lambda i: (_mod_idx(layer, _seg(i, nct, lt), 1), 0, 0))],
        out_specs=[row, row],
        out_shape=[jax.ShapeDtypeStruct((t, d), F32), jax.ShapeDtypeStruct((t, d), BF16)],
        compiler_params=_cp(("arbitrary",), 40),
        name="prenorm1",
    )(ctx_rows, lat_rows, g.reshape(1, d), mods, mods)


def _matmul(a, w, layer, tm, tn, name):
    m, k = a.shape
    n = w.shape[2]

    def body(a_ref, w_ref, o_ref, wb):
        @pl.when(pl.program_id(1) == 0)
        def _():
            wb[...] = w_ref[...].astype(BF16)

        o_ref[...] = _dot(a_ref[...], wb[...])

    return pl.pallas_call(
        body,
        grid=(n // tn, m // tm),
        in_specs=[pl.BlockSpec((tm, k), lambda j, i: (i, 0)),
                  pl.BlockSpec((None, k, tn), lambda j, i: (layer, 0, j), pipeline_mode=pl.Buffered(1))],
        out_specs=pl.BlockSpec((tm, tn), lambda j, i: (i, j)),
        out_shape=jax.ShapeDtypeStruct((m, n), F32),
        scratch_shapes=[pltpu.VMEM((k, tn), BF16)],
        compiler_params=_cp(("parallel", "arbitrary"), 48),
        name=name,
    )(a, w)


def _conv_module(zl, conv_w, conv_b, gn_g, gn_b, mix_g, dims):
    t = zl.shape[0]
    k_taps, cw = conv_w.shape
    pad = (k_taps - 1) // 2
    tl, nct, ctx_len = dims["TL"], dims["NCT"], dims["CTX"]
    assert cw // CONV_GROUPS == LANES and pad <= HALO
    assert tl % GRID_W == 0 and GRID_W % CONV_ROWS == 0 and ctx_len % CONV_ROWS == 0
    upad_rows = max((tl // GRID_W) * (GRID_W + 2 * HALO), ctx_len + 2 * HALO)

    assert (cw // LANES) % 2 == 0

    def body(za_ref, zg_ref, w_ref, b_ref, gg_ref, gb_ref, mg_ref, o_ref, upad, ybuf, ssbuf, vb0, vb1):
        i = pl.program_id(0)
        upad[...] = jnp.zeros_like(upad)
        ssbuf[...] = jnp.zeros_like(ssbuf)
        def path(seg_len):
            nseg = tl // seg_len
            stride = seg_len + 2 * HALO
            nchunk = cw // LANES

            def lanes(c):
                return pl.ds(pl.multiple_of(c * LANES, LANES), LANES)

            def conv(c, vbuf):
                ln = lanes(c)
                u = za_ref[:, ln] * jax.nn.sigmoid(zg_ref[:, ln])
                for s in range(nseg):
                    upad[s * stride + HALO:s * stride + HALO + seg_len, :] = u[s * seg_len:(s + 1) * seg_len]
                bias = b_ref[:, ln]
                for s in range(nseg):
                    for r0 in range(0, seg_len, CONV_ROWS):
                        base = s * stride + HALO + r0 - pad
                        acc = jnp.broadcast_to(bias, (CONV_ROWS, LANES))
                        for k in range(k_taps):
                            acc = acc + w_ref[pl.ds(k, 1), ln] * upad[base + k:base + k + CONV_ROWS, :]
                        vbuf[s * seg_len + r0:s * seg_len + r0 + CONV_ROWS, :] = acc

            def norm(c, vbuf):
                ln = lanes(c)
                gg = gg_ref[:, ln]
                gb = gb_ref[:, ln]
                for r0 in range(0, tl, CONV_ROWS):
                    rows = pl.ds(r0, CONV_ROWS)
                    acc = vbuf[rows, :]
                    mu = jnp.mean(acc, axis=-1, keepdims=True)
                    dv = acc - mu
                    var = jnp.mean(dv * dv, axis=-1, keepdims=True)
                    y = dv * lax.rsqrt(var + EPS) * gg + gb
                    y = y * jax.nn.sigmoid(y)
                    ybuf[rows, ln] = y
                    ssbuf[rows, :] = ssbuf[rows, :] + jnp.sum(y * y, axis=-1, keepdims=True)

            conv(0, vb0)

            def pair(m, carry):
                conv(2 * m + 1, vb1)
                norm(2 * m, vb0)
                conv(2 * m + 2, vb0)
                norm(2 * m + 1, vb1)
                return carry

            lax.fori_loop(0, nchunk // 2 - 1, pair, 0)
            conv(nchunk - 1, vb1)
            norm(nchunk - 2, vb0)
            norm(nchunk - 1, vb1)

        @pl.when(i < nct)
        def _():
            path(ctx_len)

        @pl.when(i >= nct)
        def _():
            path(GRID_W)

        inv = lax.rsqrt(ssbuf[:, 0:1] * (1.0 / cw) + EPS)
        o_ref[...] = (ybuf[...] * inv * mg_ref[...]).astype(o_ref.dtype)

    vec = pl.BlockSpec((1, cw), lambda i: (0, 0))
    return pl.pallas_call(
        body,
        grid=(t // tl,),
        in_specs=[pl.BlockSpec((tl, cw), lambda i: (i, 0)),
                  pl.BlockSpec((tl, cw), lambda i: (i, 1)),
                  pl.BlockSpec((k_taps, cw), lambda i: (0, 0)),
                  vec, vec, vec, vec],
        out_specs=pl.BlockSpec((tl, cw), lambda i: (i, 0)),
        out_shape=jax.ShapeDtypeStruct((t, cw), BF16),
        scratch_shapes=[pltpu.VMEM((upad_rows, LANES), F32),
                        pltpu.VMEM((tl, cw), F32),
                        pltpu.VMEM((tl, LANES), F32),
                        pltpu.VMEM((tl, LANES), F32),
                        pltpu.VMEM((tl, LANES), F32)],
        compiler_params=_cp(("parallel",), 40),
        name="conv_module",
    )(zl, zl, conv_w, conv_b.reshape(1, cw), gn_g.reshape(1, cw), gn_b.reshape(1, cw), mix_g.reshape(1, cw))


def _lru_gates(zl, cw, cb, wa, ba, wx, bx, lam, dims):
    t = zl.shape[0]
    ndir, kc, lw = cw.shape
    heads, hd = wa.shape[1], wa.shape[2]
    tl, nct, lt = dims["TL"], dims["NCT"], dims["LT"]
    assert hd == LANES and heads * hd == lw and ndir == 2 and kc - 1 <= SUBLANES
    nrb = t // SUBLANES
    tls = tl // SUBLANES

    def body(x_ref, p_ref, n_ref, cw_ref, cb_ref, wa_ref, ba_ref, wx_ref, bx_ref, lam_ref,
             af_ref, bf_ref, ab_ref, bb_ref, xp):
        i = pl.program_id(0)
        is_ctx = i < nct
        j = jnp.where(is_ctx, 0, (i - nct) % lt)
        first = jnp.logical_or(is_ctx, j == 0)
        last = jnp.logical_or(is_ctx, j == lt - 1)
        outs = ((af_ref, bf_ref), (ab_ref, bb_ref))

        def head(h, carry):
            ln = pl.ds(pl.multiple_of(h * LANES, LANES), LANES)
            xp[0:SUBLANES, :] = jnp.where(first, 0.0, p_ref[:, ln])
            xp[SUBLANES:SUBLANES + tl, :] = x_ref[:, ln]
            xp[SUBLANES + tl:2 * SUBLANES + tl, :] = jnp.where(last, 0.0, n_ref[:, ln])
            win = {o: xp[SUBLANES + o:SUBLANES + o + tl, :] for o in range(-(kc - 1), kc)}
            for d in range(ndir):
                xc = jnp.broadcast_to(cb_ref[pl.ds(d, 1), ln], (tl, LANES))
                for k in range(kc):
                    off = k - (kc - 1) if d == 0 else (kc - 1) - k
                    xc = xc + cw_ref[d, pl.ds(k, 1), ln] * win[off]
                xcb = xc.astype(BF16)
                r = _sigmoid(_dot(xcb, wa_ref[d, h].astype(BF16)) + ba_ref[pl.ds(d, 1), ln])
                ig = _sigmoid(_dot(xcb, wx_ref[d, h].astype(BF16)) + bx_ref[pl.ds(d, 1), ln])
                sp = jax.nn.softplus(-lam_ref[pl.ds(d, 1), ln])
                log_a = (-LRU_C * sp) * r
                a_ref, b_ref = outs[d]
                a = jnp.exp(log_a)
                a_ref[:, ln] = a
                b_ref[:, ln] = jnp.sqrt(1.0 - a * a) * (ig * xc)
            return carry

        lax.fori_loop(0, heads, head, 0)

    xcol = 2
    full2 = pl.BlockSpec((ndir, lw), lambda i: (0, 0))
    full4 = pl.BlockSpec((ndir, heads, hd, hd), lambda i: (0, 0, 0, 0))
    out = pl.BlockSpec((tl, lw), lambda i: (i, 0))
    shp = jax.ShapeDtypeStruct((t, lw), F32)
    return pl.pallas_call(
        body,
        grid=(t // tl,),
        in_specs=[pl.BlockSpec((tl, lw), lambda i: (i, xcol)),
                  pl.BlockSpec((SUBLANES, lw), lambda i: (jnp.maximum(i * tls - 1, 0), xcol)),
                  pl.BlockSpec((SUBLANES, lw), lambda i: (jnp.minimum((i + 1) * tls, nrb - 1), xcol)),
                  pl.BlockSpec((ndir, kc, lw), lambda i: (0, 0, 0)),
                  full2, full4, full2, full4, full2, full2],
        out_specs=[out, out, out, out],
        out_shape=[shp, shp, shp, shp],
        scratch_shapes=[pltpu.VMEM((tl + 2 * SUBLANES, LANES), F32)],
        compiler_params=_cp(("parallel",), 40),
        name="lru_gates",
    )(zl, zl, zl, cw, cb, wa, ba, wx, bx, lam)


def _scan(af, bf, ab, bb, dims):
    t, lw = af.shape
    hs = lw // LANES
    tl, nct, lt, nb = dims["TL"], dims["NCT"], dims["LT"], dims["B"]
    assert nct == nb

    def v(z):
        return z.reshape(t, hs, LANES)

    def fidx(b, s):
        return jnp.where(s == 0, b, nct + b * lt + s - 1)

    def bidx(b, s):
        return jnp.where(s == 0, b, nct + b * lt + lt - s)

    def body(af_ref, bf_ref, ab_ref, bb_ref, hf_ref, hb_ref, hc):
        @pl.when(pl.program_id(1) == 0)
        def _():
            hc[...] = jnp.zeros_like(hc)

        def block(q, carry):
            hf, hb = carry
            tf = pl.multiple_of(q * SUBLANES, SUBLANES)
            tb = pl.multiple_of(tl - SUBLANES - q * SUBLANES, SUBLANES)
            a_f, b_f = af_ref[pl.ds(tf, SUBLANES)], bf_ref[pl.ds(tf, SUBLANES)]
            a_b, b_b = ab_ref[pl.ds(tb, SUBLANES)], bb_ref[pl.ds(tb, SUBLANES)]
            of, ob = [], []
            for k in range(SUBLANES):
                hf = a_f[k] * hf + b_f[k]
                of.append(hf)
                kb = SUBLANES - 1 - k
                hb = a_b[kb] * hb + b_b[kb]
                ob.append(hb)
            hf_ref[pl.ds(tf, SUBLANES)] = jnp.stack(of)
            hb_ref[pl.ds(tb, SUBLANES)] = jnp.stack(ob[::-1])
            return hf, hb

        hf, hb = lax.fori_loop(0, tl // SUBLANES, block, (hc[0], hc[1]))
        hc[0] = hf
        hc[1] = hb

    fspec = pl.BlockSpec((tl, hs, LANES), lambda b, s: (fidx(b, s), 0, 0))
    bspec = pl.BlockSpec((tl, hs, LANES), lambda b, s: (bidx(b, s), 0, 0))
    shp = jax.ShapeDtypeStruct((t, hs, LANES), F32)
    hf, hb = pl.pallas_call(
        body,
        grid=(nb, 1 + lt),
        in_specs=[fspec, fspec, bspec, bspec],
        out_specs=[fspec, bspec],
        out_shape=[shp, shp],
        scratch_shapes=[pltpu.VMEM((2, hs, LANES), F32)],
        compiler_params=_cp(("parallel", "arbitrary"), 40),
        name="lru_scan",
    )(v(af), v(bf), v(ab), v(bb))
    return hf.reshape(t, lw), hb.reshape(t, lw)


def _lru_out(zl, hf, hb, mix_g, dims):
    t, lw = hf.shape
    tl = dims["TL"]
    gcol = 3

    def body(g_ref, hf_ref, hb_ref, mg_ref, o_ref):
        y = jax.nn.gelu(g_ref[...], approximate=True) * (hf_ref[...] + hb_ref[...])
        inv = lax.rsqrt(jnp.mean(y * y, axis=-1, keepdims=True) + EPS)
        o_ref[...] = (y * inv * mg_ref[...]).astype(o_ref.dtype)

    row = pl.BlockSpec((tl, lw), lambda i: (i, 0))
    return pl.pallas_call(
        body,
        grid=(t // tl,),
        in_specs=[pl.BlockSpec((tl, lw), lambda i: (i, gcol)), row, row,
                  pl.BlockSpec((1, lw), lambda i: (0, 0))],
        out_specs=row,
        out_shape=jax.ShapeDtypeStruct((t, lw), BF16),
        compiler_params=_cp(("parallel",), 40),
        name="lru_out",
    )(zl, hf, hb, mix_g.reshape(1, lw))


def _lru_pass(zl, cw, cb, wa, ba, wx, bx, lam, direction, dims, hf=None, mix_g=None):
    t = zl.shape[0]
    kc, lw = cw.shape
    heads, hd = wa.shape[0], wa.shape[1]
    tl, nct, lt, nb = dims["TL"], dims["NCT"], dims["LT"], dims["B"]
    assert hd == LANES and heads * hd == lw and kc - 1 <= SUBLANES and nct == nb and heads % 2 == 0
    nrb = t // SUBLANES
    tls = tl // SUBLANES
    _, pitch = _token_rows(lw)
    back = direction == 1
    xcol, gcol = 2, 3

    def tile(b, s):
        j = lt - s if back else s - 1
        return jnp.where(s == 0, b, nct + b * lt + j)

    def body(*refs):
        if back:
            (x_ref, p_ref, n_ref, cw_ref, cb_ref, wa_ref, ba_ref, wx_ref, bx_ref, lam_ref, g_ref, hf_ref, mg_ref,
             o_ref, xp, a_s, b_s, h_s, hc, tb0, tb1, ybuf, ssbuf) = refs
        else:
            (x_ref, p_ref, n_ref, cw_ref, cb_ref, wa_ref, ba_ref, wx_ref, bx_ref, lam_ref,
             o_ref, xp, a_s, b_s, h_s, hc, tb0, tb1) = refs
        s = pl.program_id(1)
        is_ctx = s == 0
        j = lt - s if back else s - 1
        first = jnp.logical_or(is_ctx, j == 0)
        last = jnp.logical_or(is_ctx, j == lt - 1)

        @pl.when(s == 0)
        def _():
            hc[...] = jnp.zeros_like(hc)

        def lanes(h):
            return pl.ds(pl.multiple_of(h * LANES, LANES), LANES)

        def project(h, tb):
            ln = lanes(h)
            xp[0:SUBLANES, :] = jnp.where(first, 0.0, p_ref[:, ln])
            xp[SUBLANES:SUBLANES + tl, :] = x_ref[:, ln]
            xp[SUBLANES + tl:2 * SUBLANES + tl, :] = jnp.where(last, 0.0, n_ref[:, ln])
            xc = jnp.broadcast_to(cb_ref[:, ln], (tl, LANES))
            for k in range(kc):
                off = (kc - 1) - k if back else k - (kc - 1)
                xc = xc + cw_ref[pl.ds(k, 1), ln] * xp[SUBLANES + off:SUBLANES + off + tl, :]
            xcb = xc.astype(BF16)
            tb[0] = xc
            tb[1] = _dot(xcb, wa_ref[h].astype(BF16)) + ba_ref[:, ln]
            tb[2] = _dot(xcb, wx_ref[h].astype(BF16)) + bx_ref[:, ln]

        def gate(h, tb):
            ln = lanes(h)
            log_a = (-LRU_C * jax.nn.softplus(-lam_ref[:, ln])) * _sigmoid(tb[1])
            a = jnp.exp(log_a)
            a_s[pl.ds(h, tl, stride=pitch), :] = a
            b_s[pl.ds(h, tl, stride=pitch), :] = jnp.sqrt(1.0 - a * a) * (_sigmoid(tb[2]) * tb[0])

        project(0, tb0)

        def pair(m, carry):
            project(2 * m + 1, tb1)
            gate(2 * m, tb0)
            project(2 * m + 2, tb0)
            gate(2 * m + 1, tb1)
            return carry

        lax.fori_loop(0, heads // 2 - 1, pair, 0)
        project(heads - 1, tb1)
        gate(heads - 2, tb0)
        gate(heads - 1, tb1)

        def scan(q, hcur):
            for k in range(SUBLANES):
                step = q * SUBLANES + k
                row = ((tl - 1 - step) if back else step) * pitch
                hcur = a_s[pl.ds(row, heads), :] * hcur + b_s[pl.ds(row, heads), :]
                h_s[pl.ds(row, heads), :] = hcur
            return hcur

        hc[...] = lax.fori_loop(0, tl // SUBLANES, scan, hc[...])

        if back:
            ssbuf[...] = jnp.zeros_like(ssbuf)

        def finish(h, carry):
            ln = pl.ds(pl.multiple_of(h * LANES, LANES), LANES)
            hh = h_s[pl.ds(h, tl, stride=pitch), :]
            if back:
                y = jax.nn.gelu(g_ref[:, ln], approximate=True) * (hf_ref[:, ln] + hh)
                ybuf[:, ln] = y
                ssbuf[...] = ssbuf[...] + jnp.sum(y * y, axis=-1, keepdims=True)
            else:
                o_ref[:, ln] = hh
            return carry

        lax.fori_loop(0, heads, finish, 0)
        if back:
            inv = lax.rsqrt(ssbuf[:, 0:1] * (1.0 / lw) + EPS)
            o_ref[...] = (ybuf[...] * inv * mg_ref[...]).astype(o_ref.dtype)

    def col(c):
        return pl.BlockSpec((tl, lw), lambda b, s: (tile(b, s), c))

    vec = pl.BlockSpec((1, lw), lambda b, s: (0, 0))
    mat = pl.BlockSpec((heads, hd, hd), lambda b, s: (0, 0, 0))
    in_specs = [col(xcol),
                pl.BlockSpec((SUBLANES, lw), lambda b, s: (jnp.maximum(tile(b, s) * tls - 1, 0), xcol)),
                pl.BlockSpec((SUBLANES, lw), lambda b, s: (jnp.minimum((tile(b, s) + 1) * tls, nrb - 1), xcol)),
                pl.BlockSpec((kc, lw), lambda b, s: (0, 0)), vec, mat, vec, mat, vec, vec]
    args = [zl, zl, zl, cw, cb.reshape(1, lw), wa, ba.reshape(1, lw), wx, bx.reshape(1, lw), lam.reshape(1, lw)]
    scratch = [pltpu.VMEM((tl + 2 * SUBLANES, LANES), F32),
               pltpu.VMEM((tl * pitch, LANES), F32), pltpu.VMEM((tl * pitch, LANES), F32),
               pltpu.VMEM((tl * pitch, LANES), F32), pltpu.VMEM((heads, LANES), F32),
               pltpu.VMEM((3, tl, LANES), F32), pltpu.VMEM((3, tl, LANES), F32)]
    if back:
        in_specs += [col(gcol), col(0), vec]
        args += [zl, hf, mix_g.reshape(1, lw)]
        scratch += [pltpu.VMEM((tl, lw), F32), pltpu.VMEM((tl, LANES), F32)]
    return pl.pallas_call(
        body,
        grid=(nb, 1 + lt),
        in_specs=in_specs,
        out_specs=col(0),
        out_shape=jax.ShapeDtypeStruct((t, lw), BF16 if back else F32),
        scratch_shapes=scratch,
        compiler_params=_cp(("parallel", "arbitrary"), 48),
        name="lru_bwd" if back else "lru_fwd",
    )(*args)


def _wout_residual(mix_c, mix_r, w_out, xs, mods, layer, dims):
    t, d = xs.shape
    kc = mix_c.shape[1]
    kr = mix_r.shape[1]
    assert kc == kr
    tm, tn = dims["TM"], min(d, 1024)
    nct, lt = dims["B"] * dims["CTX"] // tm, dims["L"] // tm

    def body(a1_ref, a2_ref, w1_ref, w2_ref, x_ref, g_ref, o_ref, wb1, wb2):
        @pl.when(pl.program_id(1) == 0)
        def _():
            wb1[...] = w1_ref[...].astype(BF16)
            wb2[...] = w2_ref[...].astype(BF16)

        acc = _dot(a1_ref[...], wb1[...]) + _dot(a2_ref[...], wb2[...])
        o_ref[...] = x_ref[...] + g_ref[0] * acc

    once = pl.Buffered(1)
    return pl.pallas_call(
        body,
        grid=(d // tn, t // tm),
        in_specs=[pl.BlockSpec((tm, kc), lambda j, i: (i, 0)),
                  pl.BlockSpec((tm, kr), lambda j, i: (i, 0)),
                  pl.BlockSpec((None, kc, tn), lambda j, i: (layer, 0, j), pipeline_mode=once),
                  pl.BlockSpec((None, kr, tn), lambda j, i: (layer, 1, j), pipeline_mode=once),
                  pl.BlockSpec((tm, tn), lambda j, i: (i, j)),
                  pl.BlockSpec((1, 1, tn), lambda j, i: (_mod_idx(layer, _seg(i, nct, lt), 2), 0, j))],
        out_specs=pl.BlockSpec((tm, tn), lambda j, i: (i, j)),
        out_shape=jax.ShapeDtypeStruct((t, d), F32),
        scratch_shapes=[pltpu.VMEM((kc, tn), BF16), pltpu.VMEM((kr, tn), BF16)],
        compiler_params=_cp(("parallel", "arbitrary"), 48),
        name="wout_residual",
    )(mix_c, mix_r, w_out, w_out, xs, mods)


def _router(xs, g, mods, layer, rw, rb, dims):
    t, d = xs.shape
    half = d // 2
    rows_h, pitch = _token_rows(half)
    tl, nct, lt = dims["TL"], dims["NCT"], dims["LT"]

    def body(x_ref, g_ref, sh_ref, sc_ref, rw_ref, rb_ref, hp_ref, e4_ref, p4_ref, mask_ref, cnt_ref):
        x = x_ref[...]
        ms = jnp.mean(x * x, axis=-1, keepdims=True)
        h = (x * lax.rsqrt(ms + EPS) * g_ref[...]) * (1.0 + sc_ref[0]) + sh_ref[0]
        hh = h.astype(BF16)
        hh32 = hh.astype(F32)
        words = _pack_words(h)
        for cidx in range(pitch):
            chunk = words[:, cidx * LANES:(cidx + 1) * LANES] if cidx < rows_h else jnp.zeros((tl, LANES), U32)
            hp_ref[pl.ds(cidx, tl, stride=pitch), :] = chunk

        w = rw_ref[...]
        wh = w.astype(BF16)
        wl = (w - wh.astype(F32)).astype(BF16)
        hl = (h - hh32).astype(BF16)
        logits = _dot(hh, wh) + (_dot(hh, wl) + _dot(hl, wh)) + rb_ref[...]

        lane = lax.broadcasted_iota(I32, (tl, LANES), 1)
        vals = logits
        tops, hots = [], []
        e4 = jnp.zeros((tl, LANES), I32)
        for k in range(TOP_K):
            m = jnp.max(vals, axis=-1, keepdims=True)
            sel = jnp.min(jnp.where(vals == m, lane, LANES), axis=-1, keepdims=True)
            hot = lane == sel
            tops.append(m)
            hots.append(hot)
            vals = jnp.where(hot, -jnp.inf, vals)
            e4 = jnp.where(lane == k, sel, e4)
        ex = [jnp.exp(m - tops[0]) for m in tops]
        den = ex[0]
        for e in ex[1:]:
            den = den + e
        p4 = jnp.zeros((tl, LANES), F32)
        mask = jnp.zeros((tl, LANES), F32)
        for k in range(TOP_K):
            p4 = jnp.where(lane == k, ex[k] / den, p4)
            mask = jnp.where(hots[k], 1.0, mask)
        e4_ref[...] = e4
        p4_ref[...] = p4
        mask_ref[...] = mask

        @pl.when(pl.program_id(0) == 0)
        def _():
            cnt_ref[...] = jnp.zeros_like(cnt_ref)

        cnt_ref[...] = cnt_ref[...] + jnp.sum(mask, axis=0, keepdims=True)

    lanes_out = pl.BlockSpec((tl, LANES), lambda i: (i, 0))
    return pl.pallas_call(
        body,
        grid=(t // tl,),
        in_specs=[pl.BlockSpec((tl, d), lambda i: (i, 0)),
                  pl.BlockSpec((1, d), lambda i: (0, 0)),
                  pl.BlockSpec((1, 1, d), lambda i: (_mod_idx(layer, _seg(i, nct, lt), 3), 0, 0)),
                  pl.BlockSpec((1, 1, d), lambda i: (_mod_idx(layer, _seg(i, nct, lt), 4), 0, 0)),
                  pl.BlockSpec((d, LANES), lambda i: (0, 0)),
                  pl.BlockSpec((1, LANES), lambda i: (0, 0))],
        out_specs=[pl.BlockSpec((tl * pitch, LANES), lambda i: (i, 0)), lanes_out, lanes_out, lanes_out,
                   pl.BlockSpec((SUBLANES, LANES), lambda i: (0, 0))],
        out_shape=[jax.ShapeDtypeStruct((t * pitch, LANES), U32),
                   jax.ShapeDtypeStruct((t, LANES), I32),
                   jax.ShapeDtypeStruct((t, LANES), F32),
                   jax.ShapeDtypeStruct((t, LANES), F32),
                   jax.ShapeDtypeStruct((SUBLANES, LANES), F32)],
        compiler_params=_cp(("arbitrary",), 40),
        name="router",
    )(xs, g.reshape(1, d), mods, mods, rw, rb)


def _positions(mask, e4, cnt, n_experts, n_tiles, dims):
    t = mask.shape[0]
    tl = dims["TL"]
    te = EXPERT_TILE
    ntp = -(-n_tiles // SUBLANES) * SUBLANES

    def lane_cumsum(x, lane):
        sh = 1
        while sh < LANES:
            x = x + jnp.where(lane >= sh, pltpu.roll(x, sh, axis=1), 0.0)
            sh *= 2
        return x

    def body(mask_ref, e4_ref, cnt_ref, pos_ref, tab_ref, carry):
        i = pl.program_id(0)
        lane8 = lax.broadcasted_iota(I32, (SUBLANES, LANES), 1)
        cnt = cnt_ref[...]
        gp = jnp.ceil(cnt * (1.0 / te)) * te
        inc = lane_cumsum(gp, lane8)
        off = inc - gp

        @pl.when(i == 0)
        def _():
            carry[...] = jnp.zeros_like(carry)
            lane = lax.broadcasted_iota(I32, (ntp, LANES), 1)
            tile = lax.broadcasted_iota(I32, (ntp, LANES), 0).astype(F32)
            total = jnp.max(inc[0:1], axis=-1, keepdims=True)
            start = tile * te
            used = start < total
            start_c = jnp.minimum(start, total - te)
            ex = jnp.sum(jnp.where(inc[0:1] <= start_c, 1.0, 0.0), axis=-1, keepdims=True)
            ex = jnp.minimum(ex, n_experts - 1.0)
            hot = lane.astype(F32) == ex
            cnt_e = jnp.sum(jnp.where(hot, cnt[0:1], 0.0), axis=-1, keepdims=True)
            off_e = jnp.sum(jnp.where(hot, off[0:1], 0.0), axis=-1, keepdims=True)
            valid = jnp.where(used, jnp.clip(cnt_e - (start - off_e), 0.0, te), 0.0)
            tab = jnp.where(lane == 0, ex, jnp.where(lane == 1, valid, 0.0))
            tab_ref[...] = tab.astype(I32)

        m = mask_ref[...]
        r = lax.broadcasted_iota(I32, (tl, tl), 0)
        c = lax.broadcasted_iota(I32, (tl, tl), 1)
        tri = jnp.where(r > c, 1.0, 0.0).astype(BF16)
        rank = _dot(tri, m.astype(BF16))
        posf = off[0:1] + carry[0:1] + rank
        lane = lax.broadcasted_iota(I32, (tl, LANES), 1)
        e4 = e4_ref[...]
        pos4 = jnp.zeros((tl, LANES), F32)
        for k in range(TOP_K):
            ek = jnp.sum(jnp.where(lane == k, e4, 0), axis=-1, keepdims=True)
            pk = jnp.sum(jnp.where(lane == ek, posf, 0.0), axis=-1, keepdims=True)
            pos4 = jnp.where(lane == k, pk, pos4)
        pos_ref[...] = pos4.astype(I32)
        carry[...] = carry[...] + jnp.sum(m, axis=0, keepdims=True)

    rows = pl.BlockSpec((tl, LANES), lambda i: (i, 0))
    return pl.pallas_call(
        body,
        grid=(t // tl,),
        in_specs=[rows, rows, pl.BlockSpec((SUBLANES, LANES), lambda i: (0, 0))],
        out_specs=[rows, pl.BlockSpec((ntp, LANES), lambda i: (0, 0))],
        out_shape=[jax.ShapeDtypeStruct((t, LANES), I32), jax.ShapeDtypeStruct((ntp, LANES), I32)],
        scratch_shapes=[pltpu.VMEM((SUBLANES, LANES), F32)],
        compiler_params=_cp(("arbitrary",), 32),
        name="moe_positions",
    )(mask, e4, cnt)


def _invert(pos_flat, n_slots):
    n = pos_flat.shape[0]

    def body(pos_ref, zeros_ref, inv_ref, sem):
        fill = pltpu.make_async_copy(zeros_ref, inv_ref, sem)
        fill.start()
        fill.wait()

        def put(q, c):
            inv_ref[pos_ref[q]] = lax.shift_right_logical(q, 2)
            return c

        lax.fori_loop(0, n, put, 0, unroll=8)

    assert TOP_K == 4
    return pl.pallas_call(
        body,
        in_specs=[pl.BlockSpec(memory_space=pltpu.SMEM), pl.BlockSpec(memory_space=pl.ANY)],
        out_specs=pl.BlockSpec(memory_space=pltpu.SMEM),
        out_shape=jax.ShapeDtypeStruct((n_slots,), I32),
        scratch_shapes=[pltpu.SemaphoreType.DMA(())],
        name="moe_invert",
    )(pos_flat, jnp.zeros((n_slots,), I32))


def _prep_wgu(w, layer):
    _, n_exp, d, n = w.shape
    tk = min(d, 2048)
    blk = min(2 * LANES, n)

    def body(w_ref, o_ref):
        r = lax.broadcasted_iota(I32, (blk, blk), 0)
        c = lax.broadcasted_iota(I32, (blk, blk), 1)
        src = jnp.where(c < blk // 2, 2 * c, 2 * (c - blk // 2) + 1)
        perm = jnp.where(r == src, 1.0, 0.0).astype(BF16)
        for b in range(n // blk):
            cols = slice(b * blk, (b + 1) * blk)
            o_ref[0, :, cols] = _dot(w_ref[0, :, cols].astype(BF16), perm).astype(BF16)

    return pl.pallas_call(
        body,
        grid=(n_exp, d // tk),
        in_specs=[pl.BlockSpec((None, 1, tk, n), lambda e, k: (layer, e, k, 0))],
        out_specs=pl.BlockSpec((1, tk, n), lambda e, k: (e, k, 0)),
        out_shape=jax.ShapeDtypeStruct((n_exp, d, n), BF16),
        compiler_params=_cp(("parallel", "parallel"), 40),
        name="prep_wgu",
    )(w)


def _deinterleave_bias(b):
    n_exp, n = b.shape
    blk = min(2 * LANES, n)
    return b.reshape(n_exp, n // blk, blk // 2, 2).transpose(0, 1, 3, 2).reshape(n_exp, n)


def _experts(hp, inv, tab_e, tab_v, wgu, bgu, wdn, bdn, layer, n_tiles):
    n_exp, d, de2 = wgu.shape
    half = d // 2
    de = de2 // 2
    te = EXPERT_TILE
    rows_h, pitch_h = _token_rows(half)
    blk = min(2 * LANES, de2)
    inv3 = inv.reshape(n_tiles, 1, te)

    def body(te_ref, tv_ref, cur_ref, nxt_ref, hp_ref, wgu_ref, bgu_ref, wdn_ref, bdn_ref,
             ys_ref, xbuf, wdn_b, sem):
        i = pl.program_id(0)
        slot = i % 2

        @pl.when(jnp.logical_or(i == 0, te_ref[i] != te_ref[jnp.maximum(i - 1, 0)]))
        def _():
            wdn_b[...] = wdn_ref[0, 0].astype(BF16)

        def row_copy(tok, sl, r):
            return pltpu.make_async_copy(hp_ref.at[pl.ds(tok * pitch_h, rows_h)],
                                         xbuf.at[sl, pl.ds(r * pitch_h, rows_h)], sem.at[sl])

        def issue(idx_ref, sl):
            def f(r, c):
                row_copy(idx_ref[0, 0, r], sl, r).start()
                return c

            lax.fori_loop(0, te, f, 0, unroll=8)

        def wait_rows(sl):
            def w(r, c):
                row_copy(0, sl, 0).wait()
                return c

            lax.fori_loop(0, te, w, 0, unroll=8)

        @pl.when(jnp.logical_and(i == 0, tv_ref[0] > 0))
        def _():
            issue(cur_ref, 0)

        nxt = jnp.minimum(i + 1, n_tiles - 1)

        @pl.when(jnp.logical_and(i + 1 < n_tiles, tv_ref[nxt] > 0))
        def _():
            issue(nxt_ref, 1 - slot)

        @pl.when(tv_ref[i] > 0)
        def _():
            wait_rows(slot)
            los, his = [], []
            for cidx in range(rows_h):
                lo_c, hi_c = _unpack_words(xbuf[slot, pl.ds(cidx, te, stride=pitch_h), :])
                los.append(lo_c.astype(BF16))
                his.append(hi_c.astype(BF16))
            gu = _dot(jnp.concatenate(los + his, axis=1), wgu_ref[0]) + bgu_ref[0]
            acts = []
            for b in range(de2 // blk):
                gate = jnp.minimum(gu[:, b * blk:b * blk + blk // 2], SWIGLU_LIMIT)
                up = jnp.clip(gu[:, b * blk + blk // 2:(b + 1) * blk], -SWIGLU_LIMIT, SWIGLU_LIMIT)
                acts.append(((up + 1.0) * gate * jax.nn.sigmoid(SWIGLU_ALPHA * gate)).astype(BF16))
            out = _pack_words(_dot(jnp.concatenate(acts, axis=1), wdn_b[...]) + bdn_ref[0])
            for cidx in range(pitch_h):
                chunk = out[:, cidx * LANES:(cidx + 1) * LANES] if cidx < rows_h else jnp.zeros((te, LANES), U32)
                ys_ref[pl.ds(cidx, te, stride=pitch_h), :] = chunk

        @pl.when(tv_ref[i] == 0)
        def _():
            ys_ref[...] = jnp.zeros_like(ys_ref)

    grid_spec = pltpu.PrefetchScalarGridSpec(
        num_scalar_prefetch=2,
        grid=(n_tiles,),
        in_specs=[pl.BlockSpec((1, 1, te), lambda i, e, v: (i, 0, 0), memory_space=pltpu.SMEM),
                  pl.BlockSpec((1, 1, te), lambda i, e, v: (jnp.minimum(i + 1, n_tiles - 1), 0, 0),
                               memory_space=pltpu.SMEM),
                  pl.BlockSpec(memory_space=pl.ANY),
                  pl.BlockSpec((1, d, de2), lambda i, e, v: (e[i], 0, 0)),
                  pl.BlockSpec((1, 1, de2), lambda i, e, v: (e[i], 0, 0)),
                  pl.BlockSpec((1, 1, de, d), lambda i, e, v: (layer, e[i], 0, 0)),
                  pl.BlockSpec((1, 1, d), lambda i, e, v: (e[i], 0, 0))],
        out_specs=pl.BlockSpec((te * pitch_h, LANES), lambda i, e, v: (i, 0)),
        scratch_shapes=[pltpu.VMEM((2, te * pitch_h, LANES), U32), pltpu.VMEM((de, d), BF16),
                        pltpu.SemaphoreType.DMA((2,))],
    )
    return pl.pallas_call(
        body,
        grid_spec=grid_spec,
        out_shape=jax.ShapeDtypeStruct((n_tiles * te * pitch_h, LANES), U32),
        compiler_params=_cp(("arbitrary",), 56),
        name="moe_experts",
    )(tab_e, tab_v, inv3, inv3, hp, wgu, bgu.reshape(n_exp, 1, de2), wdn, bdn.reshape(n_exp, 1, d))


def _combine(xs, ys, pos_flat, p4, mods, layer, dims, final_g, next_g):
    t, d = xs.shape
    tc = COMBINE_TILE
    nq = tc * TOP_K
    nct_rows = dims["B"] * dims["CTX"]
    nct, lt = nct_rows // tc, dims["L"] // tc
    final = final_g is not None
    skip = nct if final else 0
    n_steps = t // tc - skip
    pos3 = pos_flat.reshape(t // tc, 1, nq)
    rows_h, pitch_h = _token_rows(d // 2)

    def body(cur_ref, nxt_ref, x_ref, p_ref, g_ref, fg_ref, sh_ref, sc_ref, ys_ref, o_ref, *rest):
        h_ref = None if final else rest[0]
        gbuf, sem = rest[-2:]
        i = pl.program_id(0)
        slot = i % 2

        def row_copy(src_row, sl, dst_row):
            return pltpu.make_async_copy(ys_ref.at[pl.ds(src_row, rows_h)],
                                         gbuf.at[sl, pl.ds(dst_row, rows_h)], sem.at[sl])

        def issue(idx_ref, sl):
            def f(tok, c):
                for k in range(TOP_K):
                    row_copy(idx_ref[0, 0, tok * TOP_K + k] * pitch_h, sl, (k * tc + tok) * pitch_h).start()
                return c

            lax.fori_loop(0, tc, f, 0, unroll=2)

        @pl.when(i == 0)
        def _():
            issue(cur_ref, 0)

        @pl.when(i + 1 < n_steps)
        def _():
            issue(nxt_ref, 1 - slot)

        def w(q, c):
            row_copy(0, slot, 0).wait()
            return c

        lax.fori_loop(0, nq, w, 0, unroll=8)
        p = p_ref[...]
        pk = [p[:, k:k + 1] for k in range(TOP_K)]
        lo_cols, hi_cols = [], []
        for cidx in range(rows_h):
            y_lo = y_hi = None
            for k in range(TOP_K):
                lo, hi = _unpack_words(gbuf[slot, pl.ds(k * tc * pitch_h + cidx, tc, stride=pitch_h), :])
                y_lo = pk[k] * lo if k == 0 else y_lo + pk[k] * lo
                y_hi = pk[k] * hi if k == 0 else y_hi + pk[k] * hi
            lo_cols.append(y_lo)
            hi_cols.append(y_hi)
        xn = x_ref[...] + g_ref[0] * jnp.concatenate(lo_cols + hi_cols, axis=1)
        unit = xn * lax.rsqrt(jnp.mean(xn * xn, axis=-1, keepdims=True) + EPS)
        if final:
            o_ref[...] = unit * fg_ref[...]
        else:
            o_ref[...] = xn
            h_ref[...] = ((unit * fg_ref[...]) * (1.0 + sc_ref[0]) + sh_ref[0]).astype(h_ref.dtype)

    fg = (final_g if final else next_g).reshape(1, d)
    nl = layer if final else layer + 1
    row = pl.BlockSpec((tc, d), lambda i: (i, 0))
    shp = jax.ShapeDtypeStruct((n_steps * tc, d), F32)
    out = pl.pallas_call(
        body,
        grid=(n_steps,),
        in_specs=[pl.BlockSpec((1, 1, nq), lambda i: (i + skip, 0, 0), memory_space=pltpu.SMEM),
                  pl.BlockSpec((1, 1, nq), lambda i: (jnp.minimum(i + 1, n_steps - 1) + skip, 0, 0),
                               memory_space=pltpu.SMEM),
                  pl.BlockSpec((tc, d), lambda i: (i + skip, 0)),
                  pl.BlockSpec((tc, LANES), lambda i: (i + skip, 0)),
                  pl.BlockSpec((1, 1, d), lambda i: (_mod_idx(layer, _seg(i + skip, nct, lt), 5), 0, 0)),
                  pl.BlockSpec((1, d), lambda i: (0, 0)),
                  pl.BlockSpec((1, 1, d), lambda i: (_mod_idx(nl, _seg(i + skip, nct, lt), 0), 0, 0)),
                  pl.BlockSpec((1, 1, d), lambda i: (_mod_idx(nl, _seg(i + skip, nct, lt), 1), 0, 0)),
                  pl.BlockSpec(memory_space=pl.ANY)],
        out_specs=row if final else [row, row],
        out_shape=shp if final else [shp, jax.ShapeDtypeStruct((n_steps * tc, d), BF16)],
        scratch_shapes=[pltpu.VMEM((2, TOP_K * tc * pitch_h, LANES), U32), pltpu.SemaphoreType.DMA((2,))],
        compiler_params=_cp(("arbitrary",), 40),
        name="moe_combine",
    )(pos3, pos3, xs, p4, mods, fg, mods, mods, ys)
    return (out, None) if final else tuple(out)


def kernel(x, c, ctx, c_ctx, norm1_g, norm2_g, w_ada, b_ada, w_in, conv_w, conv_b, gn_g, gn_b, lru_conv_w, lru_conv_b, lru_w_a, lru_b_a, lru_w_x, lru_b_x, lru_lambda, mix_norm_g, w_out, router_w, router_b, exp_w_gu, exp_b_gu, exp_w_down, exp_b_down, final_norm_g):
    nb, seq, d = x.shape
    ctx_len = ctx.shape[1]
    depth = w_in.shape[0]
    cw = conv_w.shape[2]
    n_exp = router_w.shape[2]
    de = exp_w_down.shape[2]
    tl = ctx_len
    t = nb * (ctx_len + seq)
    assert nb + 1 <= MOD_ROWS and seq % tl == 0 and n_exp <= LANES
    tm = min(512, nb * ctx_len)
    assert (nb * ctx_len) % tm == 0 and seq % tm == 0
    assert (nb * ctx_len) % COMBINE_TILE == 0 and seq % COMBINE_TILE == 0
    dims = {"B": nb, "CTX": ctx_len, "L": seq, "TL": tl, "NCT": nb, "LT": seq // tl, "TM": tm}
    n_tiles = (t * TOP_K) // EXPERT_TILE + n_exp
    n_slots = n_tiles * EXPERT_TILE

    cvec = jnp.concatenate([c_ctx[None], c, jnp.zeros((MOD_ROWS - 1 - nb, d), F32)], axis=0)
    mods = _adaln(cvec, w_ada, b_ada).reshape(depth * MOD_ROWS * N_MOD, 1, d)
    xs, hl = _prenorm(ctx.reshape(nb * ctx_len, d), x.reshape(nb * seq, d), norm1_g[0], mods, 0, dims)

    for l in range(depth):
        last = l == depth - 1
        zl = _matmul(hl, w_in, l, tm, min(1024, w_in.shape[2]), "in_proj")
        mix_c = _conv_module(zl, conv_w[l], conv_b[l], gn_g[l], gn_b[l], mix_norm_g[l, :cw], dims)
        lru = [(lru_conv_w[l, k], lru_conv_b[l, k], lru_w_a[l, k], lru_b_a[l, k],
                lru_w_x[l, k], lru_b_x[l, k], lru_lambda[l, k]) for k in range(2)]
        hf = _lru_pass(zl, *lru[0], 0, dims)
        mix_r = _lru_pass(zl, *lru[1], 1, dims, hf=hf, mix_g=mix_norm_g[l, cw:])
        xs = _wout_residual(mix_c, mix_r, w_out, xs, mods, l, dims)

        rw = jnp.pad(router_w[l], ((0, 0), (0, LANES - n_exp)))
        rb = jnp.pad(router_b[l], (0, LANES - n_exp), constant_values=NEG_BIG).reshape(1, LANES)
        hp, e4, p4, mask, cnt = _router(xs, norm2_g[l], mods, l, rw, rb, dims)
        pos4, tab = _positions(mask, e4, cnt, n_exp, n_tiles, dims)
        pos_flat = pos4[:, :TOP_K].reshape(t * TOP_K)
        inv = _invert(pos_flat, n_slots)
        ys = _experts(hp, inv, tab[:n_tiles, 0], tab[:n_tiles, 1],
                      _prep_wgu(exp_w_gu, l), _deinterleave_bias(exp_b_gu[l]),
                      exp_w_down, exp_b_down[l], l, n_tiles)
        xs, hl = _combine(xs, ys, pos_flat, p4, mods, l, dims, final_norm_g if last else None,
                          None if last else norm1_g[l + 1])

    return xs.reshape(nb, seq, d)
```

```python
import jax
import jax.numpy as jnp
from jax import lax
from jax.experimental import pallas as pl
from jax.experimental.pallas import tpu as pltpu

GRID_W = 64
CONV_GROUPS = 16
TOP_K = 4
N_MOD = 6
EPS = 1e-6
LRU_C = 8.0
SWIGLU_LIMIT = 7.0
SWIGLU_ALPHA = 1.702

LANES = 128
SUBLANES = 8
MOD_ROWS = 8
HALO = 16
CONV_ROWS = 64
EXPERT_TILE = 256
COMBINE_TILE = 128
NEG_BIG = -1e30

F32 = jnp.float32
BF16 = jnp.bfloat16
U32 = jnp.uint32
I32 = jnp.int32


def _cp(sems, vmem_mb):
    return pltpu.CompilerParams(dimension_semantics=sems, vmem_limit_bytes=vmem_mb << 20)


def _seg(i, n_ctx_tiles, lat_tiles):
    return jnp.where(i < n_ctx_tiles, 0, 1 + (i - n_ctx_tiles) // lat_tiles)


def _mod_idx(layer, seg, j):
    return (layer * MOD_ROWS + seg) * N_MOD + j


def _dot(a, b):
    return jnp.dot(a, b, preferred_element_type=F32)


def _sigmoid(x):
    return 0.5 * jnp.tanh(0.5 * x) + 0.5


def _pack_words(x):
    half = x.shape[1] // 2
    bits = pltpu.bitcast(x.astype(BF16).astype(F32), U32)
    return (bits[:, half:] & jnp.uint32(0xFFFF0000)) | lax.shift_right_logical(bits[:, :half], jnp.uint32(16))


def _unpack_words(words):
    lo = pltpu.bitcast(lax.shift_left(words, jnp.uint32(16)), F32)
    hi = pltpu.bitcast(words & jnp.uint32(0xFFFF0000), F32)
    return lo, hi


def _token_rows(width):
    rows = width // LANES
    return rows, rows + 4


def _adaln(cvec, w_ada, b_ada):
    depth, d, n = w_ada.shape
    tn = min(n, 1024)

    def body(c_ref, w_ref, b_ref, o_ref):
        c = c_ref[...]
        s = (c * jax.nn.sigmoid(c)).astype(BF16)
        o_ref[0] = _dot(s, w_ref[0].astype(BF16)) + b_ref[0]

    return pl.pallas_call(
        body,
        grid=(depth, n // tn),
        in_specs=[pl.BlockSpec((MOD_ROWS, d), lambda l, j: (0, 0)),
                  pl.BlockSpec((1, d, tn), lambda l, j: (l, 0, j)),
                  pl.BlockSpec((1, 1, tn), lambda l, j: (l, 0, j))],
        out_specs=pl.BlockSpec((1, MOD_ROWS, tn), lambda l, j: (l, 0, j)),
        out_shape=jax.ShapeDtypeStruct((depth, MOD_ROWS, n), F32),
        compiler_params=_cp(("parallel", "parallel"), 48),
        name="adaln",
    )(cvec, w_ada, b_ada.reshape(depth, 1, n))


def _prenorm(ctx_rows, lat_rows, g, mods, layer, dims):
    d = lat_rows.shape[1]
    t = ctx_rows.shape[0] + lat_rows.shape[0]
    tl, nct, lt = dims["TL"], dims["NCT"], dims["LT"]

    def body(c_ref, l_ref, g_ref, sh_ref, sc_ref, x_ref, o_ref):
        x = jnp.where(pl.program_id(0) < nct, c_ref[...], l_ref[...])
        x_ref[...] = x
        ms = jnp.mean(x * x, axis=-1, keepdims=True)
        y = x * lax.rsqrt(ms + EPS) * g_ref[...]
        o_ref[...] = (y * (1.0 + sc_ref[0]) + sh_ref[0]).astype(o_ref.dtype)

    row = pl.BlockSpec((tl, d), lambda i: (i, 0))
    return pl.pallas_call(
        body,
        grid=(t // tl,),
        in_specs=[pl.BlockSpec((tl, d), lambda i: (jnp.minimum(i, nct - 1), 0)),
                  pl.BlockSpec((tl, d), lambda i: (jnp.maximum(i - nct, 0), 0)),
                  pl.BlockSpec((1, d), lambda i: (0, 0)),
                  pl.BlockSpec((1, 1, d), lambda i: (_mod_idx(layer, _seg(i, nct, lt), 0), 0, 0)),
                  pl.BlockSpec((1, 1, d), lambda i: (_mod_idx(layer, _seg(i, nct, lt), 1), 0, 0))],
        out_specs=[row, row],
        out_shape=[jax.ShapeDtypeStruct((t, d), F32), jax.ShapeDtypeStruct((t, d), BF16)],
        compiler_params=_cp(("arbitrary",), 40),
        name="prenorm1",
    )(ctx_rows, lat_rows, g.reshape(1, d), mods, mods)


def _matmul(a, w, layer, tm, tn, name):
    m, k = a.shape
    n = w.shape[2]

    def body(a_ref, w_ref, o_ref, wb):
        @pl.when(pl.program_id(1) == 0)
        def _():
            wb[...] = w_ref[...].astype(BF16)

        o_ref[...] = _dot(a_ref[...], wb[...])

    return pl.pallas_call(
        body,
        grid=(n // tn, m // tm),
        in_specs=[pl.BlockSpec((tm, k), lambda j, i: (i, 0)),
                  pl.BlockSpec((None, k, tn), lambda j, i: (layer, 0, j), pipeline_mode=pl.Buffered(1))],
        out_specs=pl.BlockSpec((tm, tn), lambda j, i: (i, j)),
        out_shape=jax.ShapeDtypeStruct((m, n), F32),
        scratch_shapes=[pltpu.VMEM((k, tn), BF16)],
        compiler_params=_cp(("parallel", "arbitrary"), 48),
        name=name,
    )(a, w)


def _conv_module(zl, conv_w, conv_b, gn_g, gn_b, mix_g, dims):
    t = zl.shape[0]
    k_taps, cw = conv_w.shape
    pad = (k_taps - 1) // 2
    tl, nct, ctx_len = dims["TL"], dims["NCT"], dims["CTX"]
    assert cw // CONV_GROUPS == LANES and pad <= HALO
    assert tl % GRID_W == 0 and GRID_W % CONV_ROWS == 0 and ctx_len % CONV_ROWS == 0
    upad_rows = max((tl // GRID_W) * (GRID_W + 2 * HALO), ctx_len + 2 * HALO)

    assert (cw // LANES) % 2 == 0

    def body(za_ref, zg_ref, w_ref, b_ref, gg_ref, gb_ref, mg_ref, o_ref, upad, ybuf, ssbuf, vb0, vb1):
        i = pl.program_id(0)
        upad[...] = jnp.zeros_like(upad)
        ssbuf[...] = jnp.zeros_like(ssbuf)
        def path(seg_len):
            nseg = tl // seg_len
            stride = seg_len + 2 * HALO
            nchunk = cw // LANES

            def lanes(c):
                return pl.ds(pl.multiple_of(c * LANES, LANES), LANES)

            def conv(c, vbuf):
                ln = lanes(c)
                u = za_ref[:, ln] * jax.nn.sigmoid(zg_ref[:, ln])
                for s in range(nseg):
                    upad[s * stride + HALO:s * stride + HALO + seg_len, :] = u[s * seg_len:(s + 1) * seg_len]
                bias = b_ref[:, ln]
                for s in range(nseg):
                    for r0 in range(0, seg_len, CONV_ROWS):
                        base = s * stride + HALO + r0 - pad
                        acc = jnp.broadcast_to(bias, (CONV_ROWS, LANES))
                        for k in range(k_taps):
                            acc = acc + w_ref[pl.ds(k, 1), ln] * upad[base + k:base + k + CONV_ROWS, :]
                        vbuf[s * seg_len + r0:s * seg_len + r0 + CONV_ROWS, :] = acc

            def norm(c, vbuf):
                ln = lanes(c)
                gg = gg_ref[:, ln]
                gb = gb_ref[:, ln]
                for r0 in range(0, tl, CONV_ROWS):
                    rows = pl.ds(r0, CONV_ROWS)
                    acc = vbuf[rows, :]
                    mu = jnp.mean(acc, axis=-1, keepdims=True)
                    dv = acc - mu
                    var = jnp.mean(dv * dv, axis=-1, keepdims=True)
                    y = dv * lax.rsqrt(var + EPS) * gg + gb
                    y = y * jax.nn.sigmoid(y)
                    ybuf[rows, ln] = y
                    ssbuf[rows, :] = ssbuf[rows, :] + jnp.sum(y * y, axis=-1, keepdims=True)

            conv(0, vb0)

            def pair(m, carry):
                conv(2 * m + 1, vb1)
                norm(2 * m, vb0)
                conv(2 * m + 2, vb0)
                norm(2 * m + 1, vb1)
                return carry

            lax.fori_loop(0, nchunk // 2 - 1, pair, 0)
            conv(nchunk - 1, vb1)
            norm(nchunk - 2, vb0)
            norm(nchunk - 1, vb1)

        @pl.when(i < nct)
        def _():
            path(ctx_len)

        @pl.when(i >= nct)
        def _():
            path(GRID_W)

        inv = lax.rsqrt(ssbuf[:, 0:1] * (1.0 / cw) + EPS)
        o_ref[...] = (ybuf[...] * inv * mg_ref[...]).astype(o_ref.dtype)

    vec = pl.BlockSpec((1, cw), lambda i: (0, 0))
    return pl.pallas_call(
        body,
        grid=(t // tl,),
        in_specs=[pl.BlockSpec((tl, cw), lambda i: (i, 0)),
                  pl.BlockSpec((tl, cw), lambda i: (i, 1)),
                  pl.BlockSpec((k_taps, cw), lambda i: (0, 0)),
                  vec, vec, vec, vec],
        out_specs=pl.BlockSpec((tl, cw), lambda i: (i, 0)),
        out_shape=jax.ShapeDtypeStruct((t, cw), BF16),
        scratch_shapes=[pltpu.VMEM((upad_rows, LANES), F32),
                        pltpu.VMEM((tl, cw), F32),
                        pltpu.VMEM((tl, LANES), F32),
                        pltpu.VMEM((tl, LANES), F32),
                        pltpu.VMEM((tl, LANES), F32)],
        compiler_params=_cp(("parallel",), 40),
        name="conv_module",
    )(zl, zl, conv_w, conv_b.reshape(1, cw), gn_g.reshape(1, cw), gn_b.reshape(1, cw), mix_g.reshape(1, cw))


def _lru_pass(zl, cw, cb, wa, ba, wx, bx, lam, direction, dims, hf=None, mix_g=None):
    t = zl.shape[0]
    kc, lw = cw.shape
    heads, hd = wa.shape[0], wa.shape[1]
    tl, nct, lt, nb = dims["TL"], dims["NCT"], dims["LT"], dims["B"]
    assert hd == LANES and heads * hd == lw and kc - 1 <= SUBLANES and nct == nb and heads % 2 == 0
    nrb = t // SUBLANES
    tls = tl // SUBLANES
    _, pitch = _token_rows(lw)
    back = direction == 1
    xcol, gcol = 2, 3

    def tile(b, s):
        j = lt - s if back else s - 1
        return jnp.where(s == 0, b, nct + b * lt + j)

    def body(*refs):
        if back:
            (x_ref, p_ref, n_ref, cw_ref, cb_ref, wa_ref, ba_ref, wx_ref, bx_ref, lam_ref, g_ref, hf_ref, mg_ref,
             o_ref, xp, a_s, b_s, h_s, hc, tb0, tb1, ybuf, ssbuf) = refs
        else:
            (x_ref, p_ref, n_ref, cw_ref, cb_ref, wa_ref, ba_ref, wx_ref, bx_ref, lam_ref,
             o_ref, xp, a_s, b_s, h_s, hc, tb0, tb1) = refs
        s = pl.program_id(1)
        is_ctx = s == 0
        j = lt - s if back else s - 1
        first = jnp.logical_or(is_ctx, j == 0)
        last = jnp.logical_or(is_ctx, j == lt - 1)

        @pl.when(s == 0)
        def _():
            hc[...] = jnp.zeros_like(hc)

        def lanes(h):
            return pl.ds(pl.multiple_of(h * LANES, LANES), LANES)

        def project(h, tb):
            ln = lanes(h)
            xp[0:SUBLANES, :] = jnp.where(first, 0.0, p_ref[:, ln])
            xp[SUBLANES:SUBLANES + tl, :] = x_ref[:, ln]
            xp[SUBLANES + tl:2 * SUBLANES + tl, :] = jnp.where(last, 0.0, n_ref[:, ln])
            xc = jnp.broadcast_to(cb_ref[:, ln], (tl, LANES))
            for k in range(kc):
                off = (kc - 1) - k if back else k - (kc - 1)
                xc = xc + cw_ref[pl.ds(k, 1), ln] * xp[SUBLANES + off:SUBLANES + off + tl, :]
            xcb = xc.astype(BF16)
            tb[0] = xc
            tb[1] = _dot(xcb, wa_ref[h].astype(BF16)) + ba_ref[:, ln]
            tb[2] = _dot(xcb, wx_ref[h].astype(BF16)) + bx_ref[:, ln]

        def gate(h, tb):
            ln = lanes(h)
            log_a = (-LRU_C * jax.nn.softplus(-lam_ref[:, ln])) * _sigmoid(tb[1])
            a = jnp.exp(log_a)
            a_s[pl.ds(h, tl, stride=pitch), :] = a
            b_s[pl.ds(h, tl, stride=pitch), :] = jnp.sqrt(1.0 - a * a) * (_sigmoid(tb[2]) * tb[0])

        project(0, tb0)

        def pair(m, carry):
            project(2 * m + 1, tb1)
            gate(2 * m, tb0)
            project(2 * m + 2, tb0)
            gate(2 * m + 1, tb1)
            return carry

        lax.fori_loop(0, heads // 2 - 1, pair, 0)
        project(heads - 1, tb1)
        gate(heads - 2, tb0)
        gate(heads - 1, tb1)

        def scan(q, hcur):
            for k in range(SUBLANES):
                step = q * SUBLANES + k
                row = ((tl - 1 - step) if back else step) * pitch
                hcur = a_s[pl.ds(row, heads), :] * hcur + b_s[pl.ds(row, heads), :]
                h_s[pl.ds(row, heads), :] = hcur
            return hcur

        hc[...] = lax.fori_loop(0, tl // SUBLANES, scan, hc[...])

        if back:
            ssbuf[...] = jnp.zeros_like(ssbuf)

        def finish(h, carry):
            ln = pl.ds(pl.multiple_of(h * LANES, LANES), LANES)
            hh = h_s[pl.ds(h, tl, stride=pitch), :]
            if back:
                y = jax.nn.gelu(g_ref[:, ln], approximate=True) * (hf_ref[:, ln] + hh)
                ybuf[:, ln] = y
                ssbuf[...] = ssbuf[...] + jnp.sum(y * y, axis=-1, keepdims=True)
            else:
                o_ref[:, ln] = hh
            return carry

        lax.fori_loop(0, heads, finish, 0)
        if back:
            inv = lax.rsqrt(ssbuf[:, 0:1] * (1.0 / lw) + EPS)
            o_ref[...] = (ybuf[...] * inv * mg_ref[...]).astype(o_ref.dtype)

    def col(c):
        return pl.BlockSpec((tl, lw), lambda b, s: (tile(b, s), c))

    vec = pl.BlockSpec((1, lw), lambda b, s: (0, 0))
    mat = pl.BlockSpec((heads, hd, hd), lambda b, s: (0, 0, 0))
    in_specs = [col(xcol),
                pl.BlockSpec((SUBLANES, lw), lambda b, s: (jnp.maximum(tile(b, s) * tls - 1, 0), xcol)),
                pl.BlockSpec((SUBLANES, lw), lambda b, s: (jnp.minimum((tile(b, s) + 1) * tls, nrb - 1), xcol)),
                pl.BlockSpec((kc, lw), lambda b, s: (0, 0)), vec, mat, vec, mat, vec, vec]
    args = [zl, zl, zl, cw, cb.reshape(1, lw), wa, ba.reshape(1, lw), wx, bx.reshape(1, lw), lam.reshape(1, lw)]
    scratch = [pltpu.VMEM((tl + 2 * SUBLANES, LANES), F32),
               pltpu.VMEM((tl * pitch, LANES), F32), pltpu.VMEM((tl * pitch, LANES), F32),
               pltpu.VMEM((tl * pitch, LANES), F32), pltpu.VMEM((heads, LANES), F32),
               pltpu.VMEM((3, tl, LANES), F32), pltpu.VMEM((3, tl, LANES), F32)]
    if back:
        in_specs += [col(gcol), col(0), vec]
        args += [zl, hf, mix_g.reshape(1, lw)]
        scratch += [pltpu.VMEM((tl, lw), F32), pltpu.VMEM((tl, LANES), F32)]
    return pl.pallas_call(
        body,
        grid=(nb, 1 + lt),
        in_specs=in_specs,
        out_specs=col(0),
        out_shape=jax.ShapeDtypeStruct((t, lw), BF16 if back else F32),
        scratch_shapes=scratch,
        compiler_params=_cp(("parallel", "arbitrary"), 48),
        name="lru_bwd" if back else "lru_fwd",
    )(*args)


def _wout_residual(mix_c, mix_r, w_out, xs, mods, layer, dims):
    t, d = xs.shape
    kc = mix_c.shape[1]
    kr = mix_r.shape[1]
    assert kc == kr
    tm, tn = dims["TM"], min(d, 1024)
    nct, lt = dims["B"] * dims["CTX"] // tm, dims["L"] // tm

    def body(a1_ref, a2_ref, w1_ref, w2_ref, x_ref, g_ref, o_ref, wb1, wb2):
        @pl.when(pl.program_id(1) == 0)
        def _():
            wb1[...] = w1_ref[...].astype(BF16)
            wb2[...] = w2_ref[...].astype(BF16)

        acc = _dot(a1_ref[...], wb1[...]) + _dot(a2_ref[...], wb2[...])
        o_ref[...] = x_ref[...] + g_ref[0] * acc

    once = pl.Buffered(1)
    return pl.pallas_call(
        body,
        grid=(d // tn, t // tm),
        in_specs=[pl.BlockSpec((tm, kc), lambda j, i: (i, 0)),
                  pl.BlockSpec((tm, kr), lambda j, i: (i, 0)),
                  pl.BlockSpec((None, kc, tn), lambda j, i: (layer, 0, j), pipeline_mode=once),
                  pl.BlockSpec((None, kr, tn), lambda j, i: (layer, 1, j), pipeline_mode=once),
                  pl.BlockSpec((tm, tn), lambda j, i: (i, j)),
                  pl.BlockSpec((1, 1, tn), lambda j, i: (_mod_idx(layer, _seg(i, nct, lt), 2), 0, j))],
        out_specs=pl.BlockSpec((tm, tn), lambda j, i: (i, j)),
        out_shape=jax.ShapeDtypeStruct((t, d), F32),
        scratch_shapes=[pltpu.VMEM((kc, tn), BF16), pltpu.VMEM((kr, tn), BF16)],
        compiler_params=_cp(("parallel", "arbitrary"), 48),
        name="wout_residual",
    )(mix_c, mix_r, w_out, w_out, xs, mods)


def _router(xs, g, mods, layer, rw, rb, dims):
    t, d = xs.shape
    half = d // 2
    rows_h, pitch = _token_rows(half)
    tl, nct, lt = dims["TL"], dims["NCT"], dims["LT"]

    def body(x_ref, g_ref, sh_ref, sc_ref, rw_ref, rb_ref, hp_ref, e4_ref, p4_ref, mask_ref, cnt_ref,
             wh_s, wl_s):
        x = x_ref[...]
        ms = jnp.mean(x * x, axis=-1, keepdims=True)
        h = (x * lax.rsqrt(ms + EPS) * g_ref[...]) * (1.0 + sc_ref[0]) + sh_ref[0]
        hh = h.astype(BF16)
        hh32 = hh.astype(F32)
        words = _pack_words(h)
        for cidx in range(pitch):
            chunk = words[:, cidx * LANES:(cidx + 1) * LANES] if cidx < rows_h else jnp.zeros((tl, LANES), U32)
            hp_ref[pl.ds(cidx, tl, stride=pitch), :] = chunk

        @pl.when(pl.program_id(0) == 0)
        def _():
            w = rw_ref[...]
            wh = w.astype(BF16)
            wh_s[...] = wh
            wl_s[...] = (w - wh.astype(F32)).astype(BF16)

        hl = (h - hh32).astype(BF16)
        logits = _dot(hh, wh_s[...]) + (_dot(hh, wl_s[...]) + _dot(hl, wh_s[...])) + rb_ref[...]

        lane = lax.broadcasted_iota(I32, (tl, LANES), 1)
        vals = logits
        tops, hots = [], []
        e4 = jnp.zeros((tl, LANES), I32)
        for k in range(TOP_K):
            m = jnp.max(vals, axis=-1, keepdims=True)
            sel = jnp.min(jnp.where(vals == m, lane, LANES), axis=-1, keepdims=True)
            hot = lane == sel
            tops.append(m)
            hots.append(hot)
            vals = jnp.where(hot, -jnp.inf, vals)
            e4 = jnp.where(lane == k, sel, e4)
        ex = [jnp.exp(m - tops[0]) for m in tops]
        den = ex[0]
        for e in ex[1:]:
            den = den + e
        p4 = jnp.zeros((tl, LANES), F32)
        mask = jnp.zeros((tl, LANES), F32)
        for k in range(TOP_K):
            p4 = jnp.where(lane == k, ex[k] / den, p4)
            mask = jnp.where(hots[k], 1.0, mask)
        e4_ref[...] = e4
        p4_ref[...] = p4
        mask_ref[...] = mask

        @pl.when(pl.program_id(0) == 0)
        def _():
            cnt_ref[...] = jnp.zeros_like(cnt_ref)

        cnt_ref[...] = cnt_ref[...] + jnp.sum(mask, axis=0, keepdims=True)

    lanes_out = pl.BlockSpec((tl, LANES), lambda i: (i, 0))
    return pl.pallas_call(
        body,
        grid=(t // tl,),
        in_specs=[pl.BlockSpec((tl, d), lambda i: (i, 0)),
                  pl.BlockSpec((1, d), lambda i: (0, 0)),
                  pl.BlockSpec((1, 1, d), lambda i: (_mod_idx(layer, _seg(i, nct, lt), 3), 0, 0)),
                  pl.BlockSpec((1, 1, d), lambda i: (_mod_idx(layer, _seg(i, nct, lt), 4), 0, 0)),
                  pl.BlockSpec((d, LANES), lambda i: (0, 0)),
                  pl.BlockSpec((1, LANES), lambda i: (0, 0))],
        out_specs=[pl.BlockSpec((tl * pitch, LANES), lambda i: (i, 0)), lanes_out, lanes_out, lanes_out,
                   pl.BlockSpec((SUBLANES, LANES), lambda i: (0, 0))],
        out_shape=[jax.ShapeDtypeStruct((t * pitch, LANES), U32),
                   jax.ShapeDtypeStruct((t, LANES), I32),
                   jax.ShapeDtypeStruct((t, LANES), F32),
                   jax.ShapeDtypeStruct((t, LANES), F32),
                   jax.ShapeDtypeStruct((SUBLANES, LANES), F32)],
        scratch_shapes=[pltpu.VMEM((d, LANES), BF16), pltpu.VMEM((d, LANES), BF16)],
        compiler_params=_cp(("arbitrary",), 40),
        name="router",
    )(xs, g.reshape(1, d), mods, mods, rw, rb)


def _positions(mask, e4, cnt, n_experts, n_tiles, dims):
    t = mask.shape[0]
    tl = dims["TL"]
    te = EXPERT_TILE
    ntp = -(-n_tiles // SUBLANES) * SUBLANES

    def lane_cumsum(x, lane):
        sh = 1
        while sh < LANES:
            x = x + jnp.where(lane >= sh, pltpu.roll(x, sh, axis=1), 0.0)
            sh *= 2
        return x

    def body(mask_ref, e4_ref, cnt_ref, pos_ref, tab_ref, carry):
        i = pl.program_id(0)
        lane8 = lax.broadcasted_iota(I32, (SUBLANES, LANES), 1)
        cnt = cnt_ref[...]
        gp = jnp.ceil(cnt * (1.0 / te)) * te
        inc = lane_cumsum(gp, lane8)
        off = inc - gp

        @pl.when(i == 0)
        def _():
            carry[...] = jnp.zeros_like(carry)
            lane = lax.broadcasted_iota(I32, (ntp, LANES), 1)
            tile = lax.broadcasted_iota(I32, (ntp, LANES), 0).astype(F32)
            total = jnp.max(inc[0:1], axis=-1, keepdims=True)
            start = tile * te
            used = start < total
            start_c = jnp.minimum(start, total - te)
            ex = jnp.sum(jnp.where(inc[0:1] <= start_c, 1.0, 0.0), axis=-1, keepdims=True)
            ex = jnp.minimum(ex, n_experts - 1.0)
            hot = lane.astype(F32) == ex
            cnt_e = jnp.sum(jnp.where(hot, cnt[0:1], 0.0), axis=-1, keepdims=True)
            off_e = jnp.sum(jnp.where(hot, off[0:1], 0.0), axis=-1, keepdims=True)
            valid = jnp.where(used, jnp.clip(cnt_e - (start - off_e), 0.0, te), 0.0)
            tab = jnp.where(lane == 0, ex, jnp.where(lane == 1, valid, 0.0))
            tab_ref[...] = tab.astype(I32)

        m = mask_ref[...]
        r = lax.broadcasted_iota(I32, (tl, tl), 0)
        c = lax.broadcasted_iota(I32, (tl, tl), 1)
        tri = jnp.where(r > c, 1.0, 0.0).astype(BF16)
        rank = _dot(tri, m.astype(BF16))
        posf = off[0:1] + carry[0:1] + rank
        lane = lax.broadcasted_iota(I32, (tl, LANES), 1)
        e4 = e4_ref[...]
        pos4 = jnp.zeros((tl, LANES), F32)
        for k in range(TOP_K):
            ek = jnp.sum(jnp.where(lane == k, e4, 0), axis=-1, keepdims=True)
            pk = jnp.sum(jnp.where(lane == ek, posf, 0.0), axis=-1, keepdims=True)
            pos4 = jnp.where(lane == k, pk, pos4)
        pos_ref[...] = pos4.astype(I32)
        carry[...] = carry[...] + jnp.sum(m, axis=0, keepdims=True)

    rows = pl.BlockSpec((tl, LANES), lambda i: (i, 0))
    return pl.pallas_call(
        body,
        grid=(t // tl,),
        in_specs=[rows, rows, pl.BlockSpec((SUBLANES, LANES), lambda i: (0, 0))],
        out_specs=[rows, pl.BlockSpec((ntp, LANES), lambda i: (0, 0))],
        out_shape=[jax.ShapeDtypeStruct((t, LANES), I32), jax.ShapeDtypeStruct((ntp, LANES), I32)],
        scratch_shapes=[pltpu.VMEM((SUBLANES, LANES), F32)],
        compiler_params=_cp(("arbitrary",), 32),
        name="moe_positions",
    )(mask, e4, cnt)


def _invert(pos_flat, n_slots):
    n = pos_flat.shape[0]

    def body(pos_ref, zeros_ref, inv_ref, sem):
        fill = pltpu.make_async_copy(zeros_ref, inv_ref, sem)
        fill.start()
        fill.wait()

        def put(q, c):
            inv_ref[pos_ref[q]] = lax.shift_right_logical(q, 2)
            return c

        lax.fori_loop(0, n, put, 0, unroll=8)

    assert TOP_K == 4
    return pl.pallas_call(
        body,
        in_specs=[pl.BlockSpec(memory_space=pltpu.SMEM), pl.BlockSpec(memory_space=pl.ANY)],
        out_specs=pl.BlockSpec(memory_space=pltpu.SMEM),
        out_shape=jax.ShapeDtypeStruct((n_slots,), I32),
        scratch_shapes=[pltpu.SemaphoreType.DMA(())],
        name="moe_invert",
    )(pos_flat, jnp.zeros((n_slots,), I32))


def _prep_wgu(w, layer):
    _, n_exp, d, n = w.shape
    tk = min(d, 2048)
    blk = min(2 * LANES, n)

    def body(w_ref, o_ref):
        r = lax.broadcasted_iota(I32, (blk, blk), 0)
        c = lax.broadcasted_iota(I32, (blk, blk), 1)
        src = jnp.where(c < blk // 2, 2 * c, 2 * (c - blk // 2) + 1)
        perm = jnp.where(r == src, 1.0, 0.0).astype(BF16)
        for b in range(n // blk):
            cols = slice(b * blk, (b + 1) * blk)
            o_ref[0, :, cols] = _dot(w_ref[0, :, cols].astype(BF16), perm).astype(BF16)

    return pl.pallas_call(
        body,
        grid=(n_exp, d // tk),
        in_specs=[pl.BlockSpec((None, 1, tk, n), lambda e, k: (layer, e, k, 0))],
        out_specs=pl.BlockSpec((1, tk, n), lambda e, k: (e, k, 0)),
        out_shape=jax.ShapeDtypeStruct((n_exp, d, n), BF16),
        compiler_params=_cp(("parallel", "parallel"), 40),
        name="prep_wgu",
    )(w)


def _deinterleave_bias(b):
    n_exp, n = b.shape
    blk = min(2 * LANES, n)
    return b.reshape(n_exp, n // blk, blk // 2, 2).transpose(0, 1, 3, 2).reshape(n_exp, n)


def _experts(hp, inv, tab_e, tab_v, wgu, bgu, wdn, bdn, layer, n_tiles):
    n_exp, d, de2 = wgu.shape
    half = d // 2
    de = de2 // 2
    te = EXPERT_TILE
    rows_h, pitch_h = _token_rows(half)
    blk = min(2 * LANES, de2)
    inv3 = inv.reshape(n_tiles, 1, te)

    def body(te_ref, tv_ref, cur_ref, nxt_ref, hp_ref, wgu_ref, bgu_ref, wdn_ref, bdn_ref,
             ys_ref, xbuf, wdn_b, sem):
        i = pl.program_id(0)
        slot = i % 2

        @pl.when(jnp.logical_or(i == 0, te_ref[i] != te_ref[jnp.maximum(i - 1, 0)]))
        def _():
            wdn_b[...] = wdn_ref[0, 0].astype(BF16)

        def row_copy(tok, sl, r):
            return pltpu.make_async_copy(hp_ref.at[pl.ds(tok * pitch_h, rows_h)],
                                         xbuf.at[sl, pl.ds(r * pitch_h, rows_h)], sem.at[sl])

        def issue(idx_ref, sl):
            def f(r, c):
                row_copy(idx_ref[0, 0, r], sl, r).start()
                return c

            lax.fori_loop(0, te, f, 0, unroll=8)

        def wait_rows(sl):
            def w(r, c):
                row_copy(0, sl, 0).wait()
                return c

            lax.fori_loop(0, te, w, 0, unroll=8)

        @pl.when(jnp.logical_and(i == 0, tv_ref[0] > 0))
        def _():
            issue(cur_ref, 0)

        nxt = jnp.minimum(i + 1, n_tiles - 1)

        @pl.when(jnp.logical_and(i + 1 < n_tiles, tv_ref[nxt] > 0))
        def _():
            issue(nxt_ref, 1 - slot)

        @pl.when(tv_ref[i] > 0)
        def _():
            wait_rows(slot)
            los, his = [], []
            for cidx in range(rows_h):
                lo_c, hi_c = _unpack_words(xbuf[slot, pl.ds(cidx, te, stride=pitch_h), :])
                los.append(lo_c.astype(BF16))
                his.append(hi_c.astype(BF16))
            gu = _dot(jnp.concatenate(los + his, axis=1), wgu_ref[0]) + bgu_ref[0]
            acts = []
            for b in range(de2 // blk):
                gate = jnp.minimum(gu[:, b * blk:b * blk + blk // 2], SWIGLU_LIMIT)
                up = jnp.clip(gu[:, b * blk + blk // 2:(b + 1) * blk], -SWIGLU_LIMIT, SWIGLU_LIMIT)
                acts.append(((up + 1.0) * gate * jax.nn.sigmoid(SWIGLU_ALPHA * gate)).astype(BF16))
            out = _pack_words(_dot(jnp.concatenate(acts, axis=1), wdn_b[...]) + bdn_ref[0])
            for cidx in range(pitch_h):
                chunk = out[:, cidx * LANES:(cidx + 1) * LANES] if cidx < rows_h else jnp.zeros((te, LANES), U32)
                ys_ref[pl.ds(cidx, te, stride=pitch_h), :] = chunk

        @pl.when(tv_ref[i] == 0)
        def _():
            ys_ref[...] = jnp.zeros_like(ys_ref)

    grid_spec = pltpu.PrefetchScalarGridSpec(
        num_scalar_prefetch=2,
        grid=(n_tiles,),
        in_specs=[pl.BlockSpec((1, 1, te), lambda i, e, v: (i, 0, 0), memory_space=pltpu.SMEM),
                  pl.BlockSpec((1, 1, te), lambda i, e, v: (jnp.minimum(i + 1, n_tiles - 1), 0, 0),
                               memory_space=pltpu.SMEM),
                  pl.BlockSpec(memory_space=pl.ANY),
                  pl.BlockSpec((1, d, de2), lambda i, e, v: (e[i], 0, 0)),
                  pl.BlockSpec((1, 1, de2), lambda i, e, v: (e[i], 0, 0)),
                  pl.BlockSpec((1, 1, de, d), lambda i, e, v: (layer, e[i], 0, 0)),
                  pl.BlockSpec((1, 1, d), lambda i, e, v: (e[i], 0, 0))],
        out_specs=pl.BlockSpec((te * pitch_h, LANES), lambda i, e, v: (i, 0)),
        scratch_shapes=[pltpu.VMEM((2, te * pitch_h, LANES), U32), pltpu.VMEM((de, d), BF16),
                        pltpu.SemaphoreType.DMA((2,))],
    )
    return pl.pallas_call(
        body,
        grid_spec=grid_spec,
        out_shape=jax.ShapeDtypeStruct((n_tiles * te * pitch_h, LANES), U32),
        compiler_params=_cp(("arbitrary",), 56),
        name="moe_experts",
    )(tab_e, tab_v, inv3, inv3, hp, wgu, bgu.reshape(n_exp, 1, de2), wdn, bdn.reshape(n_exp, 1, d))


def _combine(xs, ys, pos_flat, p4, mods, layer, dims, final_g, next_g):
    t, d = xs.shape
    tc = COMBINE_TILE
    nq = tc * TOP_K
    nct_rows = dims["B"] * dims["CTX"]
    nct, lt = nct_rows // tc, dims["L"] // tc
    final = final_g is not None
    skip = nct if final else 0
    n_steps = t // tc - skip
    pos3 = pos_flat.reshape(t // tc, 1, nq)
    rows_h, pitch_h = _token_rows(d // 2)

    def body(cur_ref, nxt_ref, x_ref, p_ref, g_ref, fg_ref, sh_ref, sc_ref, ys_ref, o_ref, *rest):
        h_ref = None if final else rest[0]
        gbuf, sem = rest[-2:]
        i = pl.program_id(0)
        slot = i % 2

        def row_copy(src_row, sl, dst_row):
            return pltpu.make_async_copy(ys_ref.at[pl.ds(src_row, rows_h)],
                                         gbuf.at[sl, pl.ds(dst_row, rows_h)], sem.at[sl])

        def issue(idx_ref, sl):
            def f(tok, c):
                for k in range(TOP_K):
                    row_copy(idx_ref[0, 0, tok * TOP_K + k] * pitch_h, sl, (k * tc + tok) * pitch_h).start()
                return c

            lax.fori_loop(0, tc, f, 0, unroll=2)

        @pl.when(i == 0)
        def _():
            issue(cur_ref, 0)

        @pl.when(i + 1 < n_steps)
        def _():
            issue(nxt_ref, 1 - slot)

        def w(q, c):
            row_copy(0, slot, 0).wait()
            return c

        lax.fori_loop(0, nq, w, 0, unroll=8)
        p = p_ref[...]
        pk = [p[:, k:k + 1] for k in range(TOP_K)]
        lo_cols, hi_cols = [], []
        for cidx in range(rows_h):
            y_lo = y_hi = None
            for k in range(TOP_K):
                lo, hi = _unpack_words(gbuf[slot, pl.ds(k * tc * pitch_h + cidx, tc, stride=pitch_h), :])
                y_lo = pk[k] * lo if k == 0 else y_lo + pk[k] * lo
                y_hi = pk[k] * hi if k == 0 else y_hi + pk[k] * hi
            lo_cols.append(y_lo)
            hi_cols.append(y_hi)
        xn = x_ref[...] + g_ref[0] * jnp.concatenate(lo_cols + hi_cols, axis=1)
        unit = xn * lax.rsqrt(jnp.mean(xn * xn, axis=-1, keepdims=True) + EPS)
        if final:
            o_ref[...] = unit * fg_ref[...]
        else:
            o_ref[...] = xn
            h_ref[...] = ((unit * fg_ref[...]) * (1.0 + sc_ref[0]) + sh_ref[0]).astype(h_ref.dtype)

    fg = (final_g if final else next_g).reshape(1, d)
    nl = layer if final else layer + 1
    row = pl.BlockSpec((tc, d), lambda i: (i, 0))
    shp = jax.ShapeDtypeStruct((n_steps * tc, d), F32)
    out = pl.pallas_call(
        body,
        grid=(n_steps,),
        in_specs=[pl.BlockSpec((1, 1, nq), lambda i: (i + skip, 0, 0), memory_space=pltpu.SMEM),
                  pl.BlockSpec((1, 1, nq), lambda i: (jnp.minimum(i + 1, n_steps - 1) + skip, 0, 0),
                               memory_space=pltpu.SMEM),
                  pl.BlockSpec((tc, d), lambda i: (i + skip, 0)),
                  pl.BlockSpec((tc, LANES), lambda i: (i + skip, 0)),
                  pl.BlockSpec((1, 1, d), lambda i: (_mod_idx(layer, _seg(i + skip, nct, lt), 5), 0, 0)),
                  pl.BlockSpec((1, d), lambda i: (0, 0)),
                  pl.BlockSpec((1, 1, d), lambda i: (_mod_idx(nl, _seg(i + skip, nct, lt), 0), 0, 0)),
                  pl.BlockSpec((1, 1, d), lambda i: (_mod_idx(nl, _seg(i + skip, nct, lt), 1), 0, 0)),
                  pl.BlockSpec(memory_space=pl.ANY)],
        out_specs=row if final else [row, row],
        out_shape=shp if final else [shp, jax.ShapeDtypeStruct((n_steps * tc, d), BF16)],
        scratch_shapes=[pltpu.VMEM((2, TOP_K * tc * pitch_h, LANES), U32), pltpu.SemaphoreType.DMA((2,))],
        compiler_params=_cp(("arbitrary",), 40),
        name="moe_combine",
    )(pos3, pos3, xs, p4, mods, fg, mods, mods, ys)
    return (out, None) if final else tuple(out)


def kernel(x, c, ctx, c_ctx, norm1_g, norm2_g, w_ada, b_ada, w_in, conv_w, conv_b, gn_g, gn_b, lru_conv_w, lru_conv_b, lru_w_a, lru_b_a, lru_w_x, lru_b_x, lru_lambda, mix_norm_g, w_out, router_w, router_b, exp_w_gu, exp_b_gu, exp_w_down, exp_b_down, final_norm_g):
    nb, seq, d = x.shape
    ctx_len = ctx.shape[1]
    depth = w_in.shape[0]
    cw = conv_w.shape[2]
    n_exp = router_w.shape[2]
    de = exp_w_down.shape[2]
    tl = ctx_len
    t = nb * (ctx_len + seq)
    assert nb + 1 <= MOD_ROWS and seq % tl == 0 and n_exp <= LANES
    tm = min(512, nb * ctx_len)
    assert (nb * ctx_len) % tm == 0 and seq % tm == 0
    assert (nb * ctx_len) % COMBINE_TILE == 0 and seq % COMBINE_TILE == 0
    dims = {"B": nb, "CTX": ctx_len, "L": seq, "TL": tl, "NCT": nb, "LT": seq // tl, "TM": tm}
    n_tiles = (t * TOP_K) // EXPERT_TILE + n_exp
    n_slots = n_tiles * EXPERT_TILE

    cvec = jnp.concatenate([c_ctx[None], c, jnp.zeros((MOD_ROWS - 1 - nb, d), F32)], axis=0)
    mods = _adaln(cvec, w_ada, b_ada).reshape(depth * MOD_ROWS * N_MOD, 1, d)
    xs, hl = _prenorm(ctx.reshape(nb * ctx_len, d), x.reshape(nb * seq, d), norm1_g[0], mods, 0, dims)

    for l in range(depth):
        last = l == depth - 1
        zl = _matmul(hl, w_in, l, tm, min(1024, w_in.shape[2]), "in_proj")
        mix_c = _conv_module(zl, conv_w[l], conv_b[l], gn_g[l], gn_b[l], mix_norm_g[l, :cw], dims)
        lru = [(lru_conv_w[l, k], lru_conv_b[l, k], lru_w_a[l, k], lru_b_a[l, k],
                lru_w_x[l, k], lru_b_x[l, k], lru_lambda[l, k]) for k in range(2)]
        hf = _lru_pass(zl, *lru[0], 0, dims)
        mix_r = _lru_pass(zl, *lru[1], 1, dims, hf=hf, mix_g=mix_norm_g[l, cw:])
        xs = _wout_residual(mix_c, mix_r, w_out, xs, mods, l, dims)

        rw = jnp.pad(router_w[l], ((0, 0), (0, LANES - n_exp)))
        rb = jnp.pad(router_b[l], (0, LANES - n_exp), constant_values=NEG_BIG).reshape(1, LANES)
        hp, e4, p4, mask, cnt = _router(xs, norm2_g[l], mods, l, rw, rb, dims)
        pos4, tab = _positions(mask, e4, cnt, n_exp, n_tiles, dims)
        pos_flat = pos4[:, :TOP_K].reshape(t * TOP_K)
        inv = _invert(pos_flat, n_slots)
        ys = _experts(hp, inv, tab[:n_tiles, 0], tab[:n_tiles, 1],
                      _prep_wgu(exp_w_gu, l), _deinterleave_bias(exp_b_gu[l]),
                      exp_w_down, exp_b_down[l], l, n_tiles)
        xs, hl = _combine(xs, ys, pos_flat, p4, mods, l, dims, final_norm_g if last else None,
                          None if last else norm1_g[l + 1])

    return xs.reshape(nb, seq, d)
```
